```python
import math
import jax, jax.numpy as jnp
from jax import lax
import numpy as np

D_MODEL = 2048
BATCH = 4
SEQ = 2048
DEPTH = 1
DEC_BATCH = 1
DEC_SEQ = 16384
PAST_LEN = 128

N_META = 16
GRID_W = 64
Q_BLOCK = 128
EPS = 1e-6

A_QK_DIM = 64
A_V_DIM = 2 * A_QK_DIM
A_HEADS = (D_MODEL // 2) // A_V_DIM
A_WIDTH = A_HEADS * A_V_DIM
B_HEAD_DIM = 128
B_HEADS = (D_MODEL // 2) // B_HEAD_DIM
B_KV_HEADS = B_HEADS // 4
B_GROUP = B_HEADS // B_KV_HEADS
B_WIDTH = B_HEADS * B_HEAD_DIM
ROPE_THETA = 10000.0
MIX_WIDTH = A_WIDTH + B_WIDTH

A_Q_COLS = 2 * A_HEADS * A_QK_DIM
A_K_COLS = 2 * A_HEADS * A_QK_DIM
A_V_COLS = A_HEADS * A_V_DIM
B_Q_COLS = B_HEADS * B_HEAD_DIM
B_K_COLS = B_KV_HEADS * B_HEAD_DIM
B_V_COLS = B_KV_HEADS * B_HEAD_DIM
IN_WIDTH = A_Q_COLS + A_K_COLS + A_V_COLS + B_Q_COLS + B_K_COLS + B_V_COLS
SPLIT_POINTS = (A_Q_COLS,
                A_Q_COLS + A_K_COLS,
                A_Q_COLS + A_K_COLS + A_V_COLS,
                A_Q_COLS + A_K_COLS + A_V_COLS + B_Q_COLS,
                A_Q_COLS + A_K_COLS + A_V_COLS + B_Q_COLS + B_K_COLS)

REL_BUCKETS = 32
REL_MAX_DIST = 128

D_FF = 256 * ((8 * D_MODEL // 3 + 255) // 256)

kernel_name = "hymba_diffattn_axialgqa_macaron_encoder"


def rms_norm(x, g):
    xf = x.astype(jnp.float32)
    y = xf * lax.rsqrt(jnp.mean(xf * xf, axis=-1, keepdims=True) + EPS)
    return (y * g.astype(jnp.float32)).astype(x.dtype)


def swiglu_ffn(x, w_in, w_out):
    gate, up = jnp.split(x @ w_in, 2, axis=-1)
    return (jax.nn.silu(gate) * up) @ w_out


def t5_bucket(rel):
    half = REL_BUCKETS // 2
    max_exact = half // 2
    n = jnp.abs(rel)
    sign_off = jnp.where(rel > 0, half, 0)
    nf = jnp.maximum(n, 1).astype(jnp.float32)
    large = max_exact + (jnp.log(nf / max_exact) / math.log(REL_MAX_DIST / max_exact)
                         * (half - max_exact)).astype(jnp.int32)
    large = jnp.minimum(large, half - 1)
    return sign_off + jnp.where(n < max_exact, n, large)


def sweep_query_blocks(block_fn, q_parts):
    L = q_parts[0].shape[-2]
    n_tok = L - N_META
    nb = n_tok // Q_BLOCK
    meta_out = block_fn(tuple(q[..., :N_META, :] for q in q_parts), jnp.arange(N_META))

    def to_blocks(q):
        r = q[..., N_META:, :]
        r = r.reshape(r.shape[:-2] + (nb, Q_BLOCK, r.shape[-1]))
        return jnp.moveaxis(r, -3, 0)

    pos = (N_META + jnp.arange(n_tok)).reshape(nb, Q_BLOCK)
    outs = lax.map(lambda a: block_fn(a[0], a[1]), (tuple(to_blocks(q) for q in q_parts), pos))
    outs = jnp.moveaxis(outs, 0, -3)
    outs = outs.reshape(outs.shape[:-3] + (n_tok, outs.shape[-1]))
    return jnp.concatenate([meta_out, outs], axis=-2)


def diff_attention(aq, ak, av, lam_params, lam_init, subln_g, rel_table):
    B, L, _ = aq.shape
    q = aq.reshape(B, L, A_HEADS, 2, A_QK_DIM).transpose(0, 2, 3, 1, 4)
    k = ak.reshape(B, L, A_HEADS, 2, A_QK_DIM).transpose(0, 2, 3, 1, 4)
    v = av.reshape(B, L, A_HEADS, A_V_DIM).transpose(0, 2, 1, 3)
    k1, k2 = k[:, :, 0], k[:, :, 1]
    lp = lam_params.astype(jnp.float32)
    lam = jnp.exp(jnp.sum(lp[0] * lp[1])) - jnp.exp(jnp.sum(lp[2] * lp[3])) + lam_init
    scale = A_QK_DIM ** -0.5
    k_pos = jnp.arange(L)

    def block_fn(qs, q_pos):
        q1b, q2b = qs
        bias = rel_table.astype(jnp.float32)[t5_bucket(k_pos[None, :] - q_pos[:, None])]
        bias = jnp.moveaxis(bias, -1, 0)[None]
        s1 = jnp.einsum('bhqd,bhkd->bhqk', q1b, k1).astype(jnp.float32) * scale + bias
        s2 = jnp.einsum('bhqd,bhkd->bhqk', q2b, k2).astype(jnp.float32) * scale + bias
        attn = jax.nn.softmax(s1, axis=-1) - lam * jax.nn.softmax(s2, axis=-1)
        return jnp.einsum('bhqk,bhkd->bhqd', attn.astype(v.dtype), v)

    out = sweep_query_blocks(block_fn, (q[:, :, 0], q[:, :, 1]))
    out = rms_norm(out, subln_g) * (1.0 - lam_init)
    return out.transpose(0, 2, 1, 3).reshape(B, L, A_WIDTH)


def apply_rope(x, cos, sin):
    xp = x.astype(jnp.float32).reshape(x.shape[:-1] + (x.shape[-1] // 2, 2))
    x0, x1 = xp[..., 0], xp[..., 1]
    out = jnp.stack([x0 * cos - x1 * sin, x0 * sin + x1 * cos], axis=-1)
    return out.reshape(x.shape).astype(x.dtype)


def axial_rope_tables(rows):
    row = jnp.repeat(jnp.arange(rows), GRID_W).astype(jnp.float32)
    col = jnp.tile(jnp.arange(GRID_W), rows).astype(jnp.float32)
    axis_dim = B_HEAD_DIM // 2
    freqs = ROPE_THETA ** (-jnp.arange(0, axis_dim, 2, dtype=jnp.float32) / axis_dim)
    ang = jnp.concatenate([row[:, None] * freqs, col[:, None] * freqs], axis=-1)
    ang = jnp.concatenate([jnp.zeros((N_META, B_HEAD_DIM // 2), jnp.float32), ang], axis=0)
    return jnp.cos(ang), jnp.sin(ang)


def gqa_axial_attention(bq, bk, bv, qk_g, cos, sin):
    B, L, _ = bq.shape
    q = bq.reshape(B, L, B_KV_HEADS, B_GROUP, B_HEAD_DIM).transpose(0, 2, 3, 1, 4)
    k = bk.reshape(B, L, B_KV_HEADS, B_HEAD_DIM).transpose(0, 2, 1, 3)
    v = bv.reshape(B, L, B_KV_HEADS, B_HEAD_DIM).transpose(0, 2, 1, 3)
    q = apply_rope(rms_norm(q, qk_g[0]), cos, sin)
    k = apply_rope(rms_norm(k, qk_g[1]), cos, sin)
    scale = B_HEAD_DIM ** -0.5

    def block_fn(qs, q_pos):
        (qb,) = qs
        s = jnp.einsum('bngqd,bnkd->bngqk', qb, k).astype(jnp.float32) * scale
        p = jax.nn.softmax(s, axis=-1)
        return jnp.einsum('bngqk,bnkd->bngqd', p.astype(v.dtype), v)

    out = sweep_query_blocks(block_fn, (q,))
    return out.transpose(0, 3, 1, 2, 4).reshape(B, L, B_WIDTH)


def encoder_trunk(x, rows, meta_tokens, rel_bias_table, ffn1_norm, ffn1_w_in, ffn1_w_out,
                  mix_norm, w_in, diff_lambda, diff_subln, qk_norm, w_out,
                  ffn2_norm, ffn2_w_in, ffn2_w_out, final_norm):
    B = x.shape[0]
    meta = jnp.broadcast_to(meta_tokens.astype(x.dtype)[None], (B, N_META, D_MODEL))
    h = jnp.concatenate([meta, x], axis=1)
    cos, sin = axial_rope_tables(rows)
    for l in range(DEPTH):
        lam_init = 0.8 - 0.6 * math.exp(-0.3 * l)
        h = h + 0.5 * swiglu_ffn(rms_norm(h, ffn1_norm[l]), ffn1_w_in[l], ffn1_w_out[l])
        u = rms_norm(h, mix_norm[l])
        aq, ak, av, bq, bk, bv = jnp.split(u @ w_in[l], SPLIT_POINTS, axis=-1)
        ya = diff_attention(aq, ak, av, diff_lambda[l], lam_init, diff_subln[l], rel_bias_table)
        yb = gqa_axial_attention(bq, bk, bv, qk_norm[l], cos, sin)
        h = h + jnp.concatenate([ya, yb], axis=-1) @ w_out[l]
        h = h + 0.5 * swiglu_ffn(rms_norm(h, ffn2_norm[l]), ffn2_w_in[l], ffn2_w_out[l])
    h = rms_norm(h, final_norm)
    return h[:, N_META:]


def setup_inputs(seed: int = 0) -> dict:
    key = jax.random.key(seed)
    ks = jax.random.split(key, 17)
    f32 = jnp.float32

    def nrm(k, shape, scale):
        return jax.random.normal(k, shape, f32) * scale

    def gain(k, shape):
        return 1.0 + 0.02 * jax.random.normal(k, shape, f32)

    return {
        "x_prompt": nrm(ks[0], (BATCH, SEQ, D_MODEL), 1.0),
        "x_sample": nrm(ks[1], (DEC_BATCH, DEC_SEQ, D_MODEL), 1.0),
        "meta_tokens": nrm(ks[2], (N_META, D_MODEL), 1.0),
        "rel_bias_table": nrm(ks[3], (REL_BUCKETS, A_HEADS), 0.5),
        "ffn1_norm": gain(ks[4], (DEPTH, D_MODEL)),
        "ffn1_w_in": nrm(ks[5], (DEPTH, D_MODEL, 2 * D_FF), D_MODEL ** -0.5),
        "ffn1_w_out": nrm(ks[6], (DEPTH, D_FF, D_MODEL), D_FF ** -0.5),
        "mix_norm": gain(ks[7], (DEPTH, D_MODEL)),
        "w_in": nrm(ks[8], (DEPTH, D_MODEL, IN_WIDTH), D_MODEL ** -0.5),
        "diff_lambda": nrm(ks[9], (DEPTH, 4, A_QK_DIM), 0.1),
        "diff_subln": gain(ks[10], (DEPTH, A_V_DIM)),
        "qk_norm": gain(ks[11], (DEPTH, 2, B_HEAD_DIM)),
        "w_out": nrm(ks[12], (DEPTH, MIX_WIDTH, D_MODEL), MIX_WIDTH ** -0.5),
        "ffn2_norm": gain(ks[13], (DEPTH, D_MODEL)),
        "ffn2_w_in": nrm(ks[14], (DEPTH, D_MODEL, 2 * D_FF), D_MODEL ** -0.5),
        "ffn2_w_out": nrm(ks[15], (DEPTH, D_FF, D_MODEL), D_FF ** -0.5),
        "final_norm": gain(ks[16], (D_MODEL,)),
    }


def reference(x_prompt, x_sample, meta_tokens, rel_bias_table, ffn1_norm, ffn1_w_in, ffn1_w_out,
              mix_norm, w_in, diff_lambda, diff_subln, qk_norm, w_out,
              ffn2_norm, ffn2_w_in, ffn2_w_out, final_norm):
    rows_prompt = x_prompt.shape[1] // GRID_W
    rows_sample = x_sample.shape[1] // GRID_W
    y_prompt = encoder_trunk(x_prompt, rows_prompt, meta_tokens, rel_bias_table, ffn1_norm,
                             ffn1_w_in, ffn1_w_out, mix_norm, w_in, diff_lambda, diff_subln,
                             qk_norm, w_out, ffn2_norm, ffn2_w_in, ffn2_w_out, final_norm)
    y_sample = encoder_trunk(x_sample, rows_sample, meta_tokens, rel_bias_table, ffn1_norm,
                             ffn1_w_in, ffn1_w_out, mix_norm, w_in, diff_lambda, diff_subln,
                             qk_norm, w_out, ffn2_norm, ffn2_w_in, ffn2_w_out, final_norm)
    return (y_prompt, y_sample)
```

```python
import functools
import math

import jax
import jax.numpy as jnp
from jax import lax
from jax.experimental import pallas as pl
from jax.experimental.pallas import tpu as pltpu

F32 = jnp.float32
BF16 = jnp.bfloat16

EPS = 1e-6
N_META = 16
GRID_W = 64
D_MODEL = 2048
A_HEADS = 8
A_QK = 64
HEAD = 128
B_HEADS = 8
B_KV = 2
B_GROUP = B_HEADS // B_KV
A_COLS = A_HEADS * HEAD
REL_BUCKETS = 32
REL_MAX_DIST = 128
ROPE_THETA = 10000.0
LAM_INIT = 0.8 - 0.6 * math.exp(-0.3 * 0)
A_SCALE = A_QK ** -0.5
B_SCALE = HEAD ** -0.5
NEG = -1e30

META_PAD = 128
REL_CLIP = 128

VMEM_LIMIT = 60 * 1024 * 1024

FFN_TM = 1024
FFN_TF = 512
FFN_TN = 512
PROJ_TM = 512
KEY_BLOCK = PROJ_TM
A_MQ = 256
B_MQ = 128


def _rms(x, g):
    return x * lax.rsqrt(jnp.mean(x * x, axis=-1, keepdims=True) + EPS) * g


def _params(n_grid_dims):
    return pltpu.CompilerParams(
        dimension_semantics=("arbitrary",) * n_grid_dims,
        vmem_limit_bytes=VMEM_LIMIT,
    )


def _ffn_kernel(x_ref, g_ref, wg_ref, wu_ref, wo_ref, fg_ref, o_ref, xn_ref, *, final_norm):
    j = pl.program_id(1)

    @pl.when(j == 0)
    def _():
        x = x_ref[...]
        xn_ref[...] = _rms(x, g_ref[...]).astype(BF16)
        o_ref[...] = x

    xn = xn_ref[...]
    gate = jnp.dot(xn, wg_ref[...], preferred_element_type=F32)
    up = jnp.dot(xn, wu_ref[...], preferred_element_type=F32)
    act = (0.5 * gate / (1.0 + jnp.exp(-gate)) * up).astype(BF16)
    d = o_ref.shape[1]
    for n0 in range(0, d, FFN_TN):
        o_ref[:, n0:n0 + FFN_TN] += jnp.dot(act, wo_ref[:, n0:n0 + FFN_TN],
                                            preferred_element_type=F32)

    if final_norm:
        @pl.when(j == pl.num_programs(1) - 1)
        def _():
            o_ref[...] = _rms(o_ref[...], fg_ref[...])


def _ffn(x, norm_g, w_in, w_out, final_g, *, tm, final_norm):
    n, d = x.shape
    dff = w_out.shape[0]
    nj = dff // FFN_TF
    return pl.pallas_call(
        functools.partial(_ffn_kernel, final_norm=final_norm),
        grid=(n // tm, nj),
        in_specs=[
            pl.BlockSpec((tm, d), lambda i, j: (i, 0)),
            pl.BlockSpec((1, d), lambda i, j: (0, 0)),
            pl.BlockSpec((d, FFN_TF), lambda i, j: (0, j)),
            pl.BlockSpec((d, FFN_TF), lambda i, j: (0, j + nj)),
            pl.BlockSpec((FFN_TF, d), lambda i, j: (j, 0)),
            pl.BlockSpec((1, d), lambda i, j: (0, 0)),
        ],
        out_specs=pl.BlockSpec((tm, d), lambda i, j: (i, 0)),
        out_shape=jax.ShapeDtypeStruct((n, d), F32),
        scratch_shapes=[pltpu.VMEM((tm, d), BF16)],
        compiler_params=_params(2),
        name="ffn_final" if final_norm else "ffn",
    )(x, norm_g, w_in, w_in, w_out, final_g)


def _proj_kernel(h_ref, g_ref, w_ref, qkg_ref, cos_ref, sin_ref,
                 qat_ref, ka_ref, vat_ref, qbt_ref, kb_ref, vbt_ref):
    tm = h_ref.shape[0]
    u = _rms(h_ref[...], g_ref[...]).astype(BF16)
    cosf = cos_ref[...]
    sinf = sin_ref[...]
    lane = lax.broadcasted_iota(jnp.int32, (tm, HEAD), 1)
    even = (lane & 1) == 0

    def proj(c0, width):
        return jnp.dot(u, w_ref[:, c0:c0 + width], preferred_element_type=F32)

    def head(y, hh):
        return y[:, hh * HEAD:(hh + 1) * HEAD]

    def norm_rope(y, g):
        y = _rms(y, g)
        swapped = jnp.where(even, pltpu.roll(y, HEAD - 1, 1), pltpu.roll(y, 1, 1))
        return y * cosf + swapped * sinf

    chunk = 4 * HEAD
    for c in range(2):
        y = proj(c * chunk, chunk)
        for hh in range(4):
            qat_ref[c * 4 + hh] = (head(y, hh) * A_SCALE).T.astype(BF16)
    for c in range(2):
        ka_ref[:, c * chunk:(c + 1) * chunk] = proj(A_COLS + c * chunk, chunk).astype(BF16)
    for c in range(2):
        y = proj(2 * A_COLS + c * chunk, chunk)
        for hh in range(4):
            vat_ref[c * 4 + hh, 0] = head(y, hh).T.astype(BF16)
    gq = qkg_ref[0:1, :]
    gk = qkg_ref[1:2, :]
    for c in range(2):
        y = proj(3 * A_COLS + c * chunk, chunk)
        for hh in range(4):
            qbt_ref[c * 4 + hh] = (norm_rope(head(y, hh), gq) * B_SCALE).T.astype(BF16)
    y = proj(4 * A_COLS, chunk)
    for n in range(B_KV):
        kb_ref[:, n * HEAD:(n + 1) * HEAD] = norm_rope(head(y, n), gk).astype(BF16)
        vbt_ref[n, 0] = head(y, B_KV + n).T.astype(BF16)


def _proj(h, norm_g, w_in, qk_g, cosf, sinf, *, tm):
    n, d = h.shape
    nt = n // tm
    ntab = cosf.shape[0] // tm
    in_w = w_in.shape[1]
    out_shape = (
        jax.ShapeDtypeStruct((A_HEADS, HEAD, n), BF16),
        jax.ShapeDtypeStruct((n, A_COLS), BF16),
        jax.ShapeDtypeStruct((A_HEADS, nt, HEAD, tm), BF16),
        jax.ShapeDtypeStruct((B_HEADS, HEAD, n), BF16),
        jax.ShapeDtypeStruct((n, B_KV * HEAD), BF16),
        jax.ShapeDtypeStruct((B_KV, nt, HEAD, tm), BF16),
    )
    return pl.pallas_call(
        _proj_kernel,
        grid=(nt,),
        in_specs=[
            pl.BlockSpec((tm, d), lambda i: (i, 0)),
            pl.BlockSpec((1, d), lambda i: (0, 0)),
            pl.BlockSpec((d, in_w), lambda i: (0, 0), pipeline_mode=pl.Buffered(1)),
            pl.BlockSpec((2, HEAD), lambda i: (0, 0)),
            pl.BlockSpec((tm, HEAD), lambda i: (i % ntab, 0)),
            pl.BlockSpec((tm, HEAD), lambda i: (i % ntab, 0)),
        ],
        out_specs=(
            pl.BlockSpec((A_HEADS, HEAD, tm), lambda i: (0, 0, i)),
            pl.BlockSpec((tm, A_COLS), lambda i: (i, 0)),
            pl.BlockSpec((A_HEADS, 1, HEAD, tm), lambda i: (0, i, 0, 0)),
            pl.BlockSpec((B_HEADS, HEAD, tm), lambda i: (0, 0, i)),
            pl.BlockSpec((tm, B_KV * HEAD), lambda i: (i, 0)),
            pl.BlockSpec((B_KV, 1, HEAD, tm), lambda i: (0, i, 0, 0)),
        ),
        out_shape=out_shape,
        compiler_params=_params(1),
        name="mix_proj",
    )(h, norm_g, w_in, qk_g, cosf, sinf)


def _online_block(s, c, vt, m, l, acc_ref):
    m_new = jnp.maximum(m, jnp.max(s, axis=0, keepdims=True) + c)
    alpha = jnp.exp(m - m_new)
    p = jnp.exp(s - (m_new - c))
    l_new = alpha * l + jnp.sum(p, axis=0, keepdims=True)
    acc_ref[...] = alpha * acc_ref[...] + jnp.dot(vt, p.astype(BF16), preferred_element_type=F32)
    return m_new, l_new


def _first_block(s, vt, acc_ref):
    m = jnp.max(s, axis=0, keepdims=True)
    p = jnp.exp(s - m)
    acc_ref[...] = jnp.dot(vt, p.astype(BF16), preferred_element_type=F32)
    return m, jnp.sum(p, axis=0, keepdims=True)


def _attn_a_kernel(cfar_ref, lam_ref, g_ref, qt_ref, k_ref, vt_ref, km_ref, vmt_ref, bm_ref,
                   band_ref, o_ref, acc_ref):
    h = pl.program_id(1)
    i = pl.program_id(2)
    mq = qt_ref.shape[1]
    nk = vt_ref.shape[0]
    kb_size = vt_ref.shape[2]

    qt = qt_ref[...]
    row = lax.broadcasted_iota(jnp.int32, qt.shape, 0)
    zero = jnp.zeros_like(qt)
    qq = jnp.concatenate([jnp.where(row < A_QK, qt, zero), jnp.where(row >= A_QK, qt, zero)], axis=1)

    def scores(k):
        return jnp.dot(k, qq, preferred_element_type=F32)

    def twice(b):
        return jnp.concatenate([b, b], axis=1)

    m, l = _first_block(scores(km_ref[...]) + twice(bm_ref[...]), vmt_ref[...], acc_ref)

    def key_block(kb):
        start = pl.multiple_of(kb * kb_size, kb_size)
        return scores(k_ref[pl.ds(start, kb_size), :])

    def far_body(c):
        def body(kb, carry):
            return _online_block(key_block(kb), c, vt_ref[kb], *carry, acc_ref)
        return body

    def near_body(kb, carry):
        band = band_ref[2 * kb - i + 2]
        return _online_block(key_block(kb) + twice(band), 0.0, vt_ref[kb], *carry, acc_ref)

    lo = jnp.maximum((i - 1) // 2, 0)
    hi = jnp.minimum((i + 1) // 2 + 1, nk)
    carry = lax.fori_loop(0, lo, far_body(cfar_ref[2 * h]), (m, l))
    carry = lax.fori_loop(lo, hi, near_body, carry)
    m, l = lax.fori_loop(hi, nk, far_body(cfar_ref[2 * h + 1]), carry)

    lp = lam_ref[...]
    lam = (jnp.exp(jnp.sum(lp[0:1] * lp[1:2], axis=1, keepdims=True))
           - jnp.exp(jnp.sum(lp[2:3] * lp[3:4], axis=1, keepdims=True)) + LAM_INIT)
    r = 1.0 / l
    acc = acc_ref[...]
    o = acc[:, :mq] * r[:, :mq] - lam * (acc[:, mq:] * r[:, mq:])
    o_ref[...] = (_rms(o.T, g_ref[...]) * (1.0 - LAM_INIT)).astype(BF16)


def _attn_a(cfar, lam_p, subln_g, qat, ka, vat, kma, vmat, bias_meta, bands, *, batch):
    n = ka.shape[0]
    lr = n // batch
    nk = lr // KEY_BLOCK
    nq = lr // A_MQ
    return pl.pallas_call(
        _attn_a_kernel,
        grid=(batch, A_HEADS, nq),
        in_specs=[
            pl.BlockSpec(memory_space=pltpu.SMEM),
            pl.BlockSpec((4, A_QK), lambda b, h, i: (0, 0)),
            pl.BlockSpec((1, HEAD), lambda b, h, i: (0, 0)),
            pl.BlockSpec((None, HEAD, A_MQ), lambda b, h, i: (h, 0, b * nq + i)),
            pl.BlockSpec((lr, HEAD), lambda b, h, i: (b, h)),
            pl.BlockSpec((None, nk, HEAD, KEY_BLOCK), lambda b, h, i: (h, b, 0, 0)),
            pl.BlockSpec((META_PAD, HEAD), lambda b, h, i: (0, h)),
            pl.BlockSpec((None, None, HEAD, META_PAD), lambda b, h, i: (h, 0, 0, 0)),
            pl.BlockSpec((None, None, META_PAD, A_MQ), lambda b, h, i: (h, jnp.minimum(i, 1), 0, 0)),
            pl.BlockSpec((None, 4, KEY_BLOCK, A_MQ), lambda b, h, i: (h, 0, 0, 0)),
        ],
        out_specs=pl.BlockSpec((A_MQ, HEAD), lambda b, h, i: (b * nq + i, h)),
        out_shape=jax.ShapeDtypeStruct((n, A_COLS), BF16),
        scratch_shapes=[pltpu.VMEM((HEAD, 2 * A_MQ), F32)],
        compiler_params=_params(3),
        name="attn_a",
    )(cfar, lam_p, subln_g, qat, ka, vat, kma, vmat, bias_meta, bands)


def _attn_b_kernel(qt_ref, k_ref, vt_ref, km_ref, vmt_ref, mask_ref, o_ref, acc_ref):
    mq = qt_ref.shape[2]
    nk = vt_ref.shape[0]
    kb_size = vt_ref.shape[2]
    qq = jnp.concatenate([qt_ref[g] for g in range(B_GROUP)], axis=1)

    def scores(k):
        return jnp.dot(k, qq, preferred_element_type=F32)

    m, l = _first_block(scores(km_ref[...]) + mask_ref[...], vmt_ref[...], acc_ref)

    def body(kb, carry):
        start = pl.multiple_of(kb * kb_size, kb_size)
        s = scores(k_ref[pl.ds(start, kb_size), :])
        return _online_block(s, 0.0, vt_ref[kb], *carry, acc_ref)

    m, l = lax.fori_loop(0, nk, body, (m, l))
    o = acc_ref[...] * (1.0 / l)
    for g in range(B_GROUP):
        o_ref[:, g * HEAD:(g + 1) * HEAD] = o[:, g * mq:(g + 1) * mq].T.astype(BF16)


def _attn_b(qbt, kb, vbt, kmb, vmbt, mask, *, batch):
    n = kb.shape[0]
    lr = n // batch
    nk = lr // KEY_BLOCK
    nq = lr // B_MQ
    return pl.pallas_call(
        _attn_b_kernel,
        grid=(batch, B_KV, nq),
        in_specs=[
            pl.BlockSpec((B_GROUP, HEAD, B_MQ), lambda b, n_, i: (n_, 0, b * nq + i)),
            pl.BlockSpec((lr, HEAD), lambda b, n_, i: (b, n_)),
            pl.BlockSpec((None, nk, HEAD, KEY_BLOCK), lambda b, n_, i: (n_, b, 0, 0)),
            pl.BlockSpec((META_PAD, HEAD), lambda b, n_, i: (0, n_)),
            pl.BlockSpec((None, None, HEAD, META_PAD), lambda b, n_, i: (n_, 0, 0, 0)),
            pl.BlockSpec((META_PAD, B_GROUP * B_MQ), lambda b, n_, i: (0, 0)),
        ],
        out_specs=pl.BlockSpec((B_MQ, B_GROUP * HEAD), lambda b, n_, i: (b * nq + i, n_)),
        out_shape=jax.ShapeDtypeStruct((n, B_HEADS * HEAD), BF16),
        scratch_shapes=[pltpu.VMEM((HEAD, B_GROUP * B_MQ), F32)],
        compiler_params=_params(3),
        name="attn_b",
    )(qbt, kb, vbt, kmb, vmbt, mask)


def _out_kernel(h_ref, ya_ref, yb_ref, w_ref, o_ref):
    o_ref[...] = (h_ref[...]
                  + jnp.dot(ya_ref[...], w_ref[:A_COLS, :], preferred_element_type=F32)
                  + jnp.dot(yb_ref[...], w_ref[A_COLS:, :], preferred_element_type=F32))


def _out_proj(h, ya, yb, w_out, *, tm):
    n, d = h.shape
    return pl.pallas_call(
        _out_kernel,
        grid=(n // tm,),
        in_specs=[
            pl.BlockSpec((tm, d), lambda i: (i, 0)),
            pl.BlockSpec((tm, A_COLS), lambda i: (i, 0)),
            pl.BlockSpec((tm, B_HEADS * HEAD), lambda i: (i, 0)),
            pl.BlockSpec(w_out.shape, lambda i: (0, 0), pipeline_mode=pl.Buffered(1)),
        ],
        out_specs=pl.BlockSpec((tm, d), lambda i: (i, 0)),
        out_shape=jax.ShapeDtypeStruct((n, d), F32),
        compiler_params=_params(1),
        name="out_proj",
    )(h, ya, yb, w_out)


def _t5_bucket(rel):
    half = REL_BUCKETS // 2
    max_exact = half // 2
    n = jnp.abs(rel)
    sign_off = jnp.where(rel > 0, half, 0)
    nf = jnp.maximum(n, 1).astype(F32)
    large = max_exact + (jnp.log(nf / max_exact) / math.log(REL_MAX_DIST / max_exact)
                         * (half - max_exact)).astype(jnp.int32)
    large = jnp.minimum(large, half - 1)
    return sign_off + jnp.where(n < max_exact, n, large)


def _bias_tables(rel_table):
    rel1d = jnp.arange(-REL_CLIP, REL_CLIP + 1)
    t1d = rel_table.astype(F32)[_t5_bucket(rel1d)].T

    def lookup(rel):
        return t1d[:, jnp.clip(rel, -REL_CLIP, REL_CLIP) + REL_CLIP]

    kk = jnp.arange(KEY_BLOCK)[None, :, None]
    qq = jnp.arange(A_MQ)[None, None, :]
    delta = (jnp.arange(4) * A_MQ - KEY_BLOCK)[:, None, None]
    bands = lookup(delta + kk - qq)

    c_lo = t1d[:, 0]
    c_hi = t1d[:, -1]
    cfar = jnp.stack([c_lo, c_hi], axis=1).reshape(-1)

    j = jnp.arange(META_PAD)[:, None]
    t = jnp.arange(A_MQ)[None, :]
    first = lookup(j - (N_META + t))
    rest = jnp.broadcast_to(c_lo[:, None, None], first.shape)
    bias_meta = jnp.where((j < N_META)[None], jnp.stack([first, rest], axis=1), NEG)
    return bands, cfar, bias_meta


def _rope_tables(lr):
    rows = lr // GRID_W
    row = jnp.repeat(jnp.arange(rows), GRID_W).astype(F32)
    col = jnp.tile(jnp.arange(GRID_W), rows).astype(F32)
    axis_dim = HEAD // 2
    freqs = ROPE_THETA ** (-jnp.arange(0, axis_dim, 2, dtype=F32) / axis_dim)
    ang = jnp.concatenate([row[:, None] * freqs, col[:, None] * freqs], axis=-1)
    cos, sin = jnp.cos(ang), jnp.sin(ang)
    cosf = jnp.repeat(cos, 2, axis=-1)
    sinf = jnp.stack([-sin, sin], axis=-1).reshape(lr, HEAD)
    return cosf, sinf


def kernel(x_prompt, x_sample, meta_tokens, rel_bias_table, ffn1_norm, ffn1_w_in, ffn1_w_out,
           mix_norm, w_in, diff_lambda, diff_subln, qk_norm, w_out,
           ffn2_norm, ffn2_w_in, ffn2_w_out, final_norm):
    w1i, w1o = ffn1_w_in[0].astype(BF16), ffn1_w_out[0].astype(BF16)
    w2i, w2o = ffn2_w_in[0].astype(BF16), ffn2_w_out[0].astype(BF16)
    wi, wo = w_in[0].astype(BF16), w_out[0].astype(BF16)
    g1, gm, g2 = ffn1_norm, mix_norm, ffn2_norm
    gf = final_norm[None, :]
    qkg = qk_norm[0]
    lam_p = diff_lambda[0]
    subln = diff_subln

    bands, cfar, bias_meta = _bias_tables(rel_bias_table)
    mask_b = jnp.where(jnp.arange(META_PAD)[:, None] < N_META, 0.0, NEG).astype(F32)
    mask_b = jnp.broadcast_to(mask_b, (META_PAD, B_GROUP * B_MQ))

    xm = jnp.zeros((META_PAD, D_MODEL), F32).at[:N_META].set(meta_tokens)
    hm = _ffn(xm, g1, w1i, w1o, gf, tm=META_PAD, final_norm=False)
    ones = jnp.ones((META_PAD, HEAD), F32)
    _, kma, vmat, _, kmb, vmbt = _proj(hm, gm, wi, qkg, ones, jnp.zeros_like(ones), tm=META_PAD)

    def trunk(x):
        batch, lr, d = x.shape
        h = _ffn(x.reshape(batch * lr, d), g1, w1i, w1o, gf, tm=FFN_TM, final_norm=False)
        cosf, sinf = _rope_tables(lr)
        qat, ka, vat, qbt, kb, vbt = _proj(h, gm, wi, qkg, cosf, sinf, tm=PROJ_TM)
        ya = _attn_a(cfar, lam_p, subln, qat, ka, vat, kma, vmat, bias_meta, bands, batch=batch)
        yb = _attn_b(qbt, kb, vbt, kmb, vmbt, mask_b, batch=batch)
        h = _out_proj(h, ya, yb, wo, tm=PROJ_TM)
        y = _ffn(h, g2, w2i, w2o, gf, tm=FFN_TM, final_norm=True)
        return y.reshape(batch, lr, d)

    return trunk(x_prompt), trunk(x_sample)
```

```python
import functools
import math

import jax
import jax.numpy as jnp
from jax import lax
from jax.experimental import pallas as pl
from jax.experimental.pallas import tpu as pltpu

F32 = jnp.float32
BF16 = jnp.bfloat16

EPS = 1e-6
N_META = 16
GRID_W = 64
D_MODEL = 2048
A_HEADS = 8
A_QK = 64
HEAD = 128
B_HEADS = 8
B_KV = 2
B_GROUP = B_HEADS // B_KV
A_COLS = A_HEADS * HEAD
REL_BUCKETS = 32
REL_MAX_DIST = 128
ROPE_THETA = 10000.0
LAM_INIT = 0.8 - 0.6 * math.exp(-0.3 * 0)
LOG2E = math.log2(math.e)
A_QSCALE = A_QK ** -0.5 * LOG2E
B_QSCALE = HEAD ** -0.5 * LOG2E
NEG = -1e30

META_PAD = 128
REL_CLIP = 128

VMEM_LIMIT = 60 * 1024 * 1024

FFN_TM = 1024
FFN_TF = 512
FFN_TN = 512
PROJ_TM = 512
KEY_BLOCK = PROJ_TM
A_MQ = 256
B_MQ = 128
N_BANDS = 4


def _rms(x, g):
    return x * lax.rsqrt(jnp.mean(x * x, axis=-1, keepdims=True) + EPS) * g


def _params(n_grid_dims):
    return pltpu.CompilerParams(
        dimension_semantics=("arbitrary",) * n_grid_dims,
        vmem_limit_bytes=VMEM_LIMIT,
    )


def _ffn_kernel(x_ref, g_ref, wg_ref, wu_ref, wo_ref, fg_ref, o_ref, xn_ref, *, final_norm):
    j = pl.program_id(1)

    @pl.when(j == 0)
    def _():
        x = x_ref[...]
        xn_ref[...] = _rms(x, g_ref[...]).astype(BF16)
        o_ref[...] = x

    xn = xn_ref[...]
    gate = jnp.dot(xn, wg_ref[...], preferred_element_type=F32)
    up = jnp.dot(xn, wu_ref[...], preferred_element_type=F32)
    act = (0.5 * gate / (1.0 + jnp.exp(-gate)) * up).astype(BF16)
    d = o_ref.shape[1]
    for n0 in range(0, d, FFN_TN):
        o_ref[:, n0:n0 + FFN_TN] += jnp.dot(act, wo_ref[:, n0:n0 + FFN_TN],
                                            preferred_element_type=F32)

    if final_norm:
        @pl.when(j == pl.num_programs(1) - 1)
        def _():
            o_ref[...] = _rms(o_ref[...], fg_ref[...])


def _ffn(x, norm_g, w_in, w_out, final_g, *, tm, final_norm):
    n, d = x.shape
    dff = w_out.shape[0]
    nj = dff // FFN_TF
    return pl.pallas_call(
        functools.partial(_ffn_kernel, final_norm=final_norm),
        grid=(n // tm, nj),
        in_specs=[
            pl.BlockSpec((tm, d), lambda i, j: (i, 0)),
            pl.BlockSpec((1, d), lambda i, j: (0, 0)),
            pl.BlockSpec((d, FFN_TF), lambda i, j: (0, j)),
            pl.BlockSpec((d, FFN_TF), lambda i, j: (0, j + nj)),
            pl.BlockSpec((FFN_TF, d), lambda i, j: (j, 0)),
            pl.BlockSpec((1, d), lambda i, j: (0, 0)),
        ],
        out_specs=pl.BlockSpec((tm, d), lambda i, j: (i, 0)),
        out_shape=jax.ShapeDtypeStruct((n, d), F32),
        scratch_shapes=[pltpu.VMEM((tm, d), BF16)],
        compiler_params=_params(2),
        name="ffn_final" if final_norm else "ffn",
    )(x, norm_g, w_in, w_in, w_out, final_g)


def _proj_kernel(h_ref, g_ref, w_ref, qkg_ref, cos_ref, sin_ref,
                 qat_ref, ka_ref, vat_ref, qbt_ref, kb_ref, vbt_ref):
    tm = h_ref.shape[0]
    u = _rms(h_ref[...], g_ref[...]).astype(BF16)
    cosf = cos_ref[...]
    sinf = sin_ref[...]
    lane = lax.broadcasted_iota(jnp.int32, (tm, HEAD), 1)
    even = (lane & 1) == 0

    def proj(c0, width):
        return jnp.dot(u, w_ref[:, c0:c0 + width], preferred_element_type=F32)

    def head(y, hh):
        return y[:, hh * HEAD:(hh + 1) * HEAD]

    def norm_rope(y, g):
        y = _rms(y, g)
        swapped = jnp.where(even, pltpu.roll(y, HEAD - 1, 1), pltpu.roll(y, 1, 1))
        return y * cosf + swapped * sinf

    chunk = 4 * HEAD
    for c in range(2):
        y = proj(c * chunk, chunk)
        for hh in range(4):
            qat_ref[c * 4 + hh] = (head(y, hh) * A_QSCALE).T.astype(BF16)
    for c in range(2):
        ka_ref[:, c * chunk:(c + 1) * chunk] = proj(A_COLS + c * chunk, chunk).astype(BF16)
    for c in range(2):
        y = proj(2 * A_COLS + c * chunk, chunk)
        for hh in range(4):
            vat_ref[c * 4 + hh, 0] = head(y, hh).T.astype(BF16)
    gq = qkg_ref[0:1, :]
    gk = qkg_ref[1:2, :]
    for c in range(2):
        y = proj(3 * A_COLS + c * chunk, chunk)
        for hh in range(4):
            qbt_ref[c * 4 + hh] = (norm_rope(head(y, hh), gq) * B_QSCALE).T.astype(BF16)
    y = proj(4 * A_COLS, chunk)
    for n in range(B_KV):
        kb_ref[:, n * HEAD:(n + 1) * HEAD] = norm_rope(head(y, n), gk).astype(BF16)
        vbt_ref[n, 0] = head(y, B_KV + n).T.astype(BF16)


def _proj(h, norm_g, w_in, qk_g, cosf, sinf, *, tm):
    n, d = h.shape
    nt = n // tm
    ntab = cosf.shape[0] // tm
    in_w = w_in.shape[1]
    out_shape = (
        jax.ShapeDtypeStruct((A_HEADS, HEAD, n), BF16),
        jax.ShapeDtypeStruct((n, A_COLS), BF16),
        jax.ShapeDtypeStruct((A_HEADS, nt, HEAD, tm), BF16),
        jax.ShapeDtypeStruct((B_HEADS, HEAD, n), BF16),
        jax.ShapeDtypeStruct((n, B_KV * HEAD), BF16),
        jax.ShapeDtypeStruct((B_KV, nt, HEAD, tm), BF16),
    )
    return pl.pallas_call(
        _proj_kernel,
        grid=(nt,),
        in_specs=[
            pl.BlockSpec((tm, d), lambda i: (i, 0)),
            pl.BlockSpec((1, d), lambda i: (0, 0)),
            pl.BlockSpec((d, in_w), lambda i: (0, 0), pipeline_mode=pl.Buffered(1)),
            pl.BlockSpec((2, HEAD), lambda i: (0, 0)),
            pl.BlockSpec((tm, HEAD), lambda i: (i % ntab, 0)),
            pl.BlockSpec((tm, HEAD), lambda i: (i % ntab, 0)),
        ],
        out_specs=(
            pl.BlockSpec((A_HEADS, HEAD, tm), lambda i: (0, 0, i)),
            pl.BlockSpec((tm, A_COLS), lambda i: (i, 0)),
            pl.BlockSpec((A_HEADS, 1, HEAD, tm), lambda i: (0, i, 0, 0)),
            pl.BlockSpec((B_HEADS, HEAD, tm), lambda i: (0, 0, i)),
            pl.BlockSpec((tm, B_KV * HEAD), lambda i: (i, 0)),
            pl.BlockSpec((B_KV, 1, HEAD, tm), lambda i: (0, i, 0, 0)),
        ),
        out_shape=out_shape,
        compiler_params=_params(1),
        name="mix_proj",
    )(h, norm_g, w_in, qk_g, cosf, sinf)


def _online_block(s, c, vt, m, l, acc_ref):
    m_new = jnp.maximum(m, jnp.max(s, axis=0, keepdims=True) + c)
    alpha = jnp.exp2(m - m_new)
    p = jnp.exp2(s - (m_new - c))
    l_new = alpha * l + jnp.sum(p, axis=0, keepdims=True)
    acc_ref[...] = alpha * acc_ref[...] + jnp.dot(vt, p.astype(BF16), preferred_element_type=F32)
    return m_new, l_new


def _first_block(s, vt, acc_ref):
    m = jnp.max(s, axis=0, keepdims=True)
    p = jnp.exp2(s - m)
    acc_ref[...] = jnp.dot(vt, p.astype(BF16), preferred_element_type=F32)
    return m, jnp.sum(p, axis=0, keepdims=True)


def _pipelined_blocks(qq, k_ref, vt_ref, c_of, m, l, acc_ref, s_buf, p_buf):
    nk = vt_ref.shape[0]
    kb_size = vt_ref.shape[2]
    assert nk % 2 == 0 and nk >= 2

    def stage_scores(j, slot, m):
        start = pl.multiple_of(j * kb_size, kb_size)
        s = jnp.dot(k_ref[pl.ds(start, kb_size), :], qq, preferred_element_type=F32)
        s_buf[slot] = s
        c = c_of(j)
        m_new = jnp.maximum(m, jnp.max(s, axis=0, keepdims=True) + c)
        return m_new, jnp.exp2(m - m_new), m_new - c

    def stage_exp(slot, off, alpha, l):
        p = jnp.exp2(s_buf[slot] - off)
        p_buf[slot] = p.astype(BF16)
        return alpha * l + jnp.sum(p, axis=0, keepdims=True)

    def stage_values(j, slot, alpha):
        acc_ref[...] = alpha * acc_ref[...] + jnp.dot(vt_ref[j], p_buf[slot],
                                                      preferred_element_type=F32)

    m, a0, o0 = stage_scores(0, 0, m)
    l = stage_exp(0, o0, a0, l)
    m, a1, o1 = stage_scores(1, 1, m)

    def pair(g, carry):
        m, l, a_pp, a_p, o_p = carry
        j = 2 * g
        stage_values(j - 2, 0, a_pp)
        l = stage_exp(1, o_p, a_p, l)
        m, a_j, o_j = stage_scores(j, 0, m)
        stage_values(j - 1, 1, a_p)
        l = stage_exp(0, o_j, a_j, l)
        m, a_j1, o_j1 = stage_scores(j + 1, 1, m)
        return m, l, a_j, a_j1, o_j1

    m, l, a_pp, a_p, o_p = lax.fori_loop(1, nk // 2, pair, (m, l, a0, a1, o1))
    stage_values(nk - 2, 0, a_pp)
    l = stage_exp(1, o_p, a_p, l)
    stage_values(nk - 1, 1, a_p)
    return m, l


def _attn_a_kernel(cfar_ref, lam_ref, g_ref, qt_ref, k_ref, vt_ref, km_ref, vmt_ref, bm_ref,
                   band_ref, o_ref, acc_ref, s_buf, p_buf):
    h = pl.program_id(1)
    i = pl.program_id(2)
    mq = qt_ref.shape[1]
    nk = vt_ref.shape[0]
    kb_size = vt_ref.shape[2]

    qt = qt_ref[...]
    row = lax.broadcasted_iota(jnp.int32, qt.shape, 0)
    zero = jnp.zeros_like(qt)
    qq = jnp.concatenate([jnp.where(row < A_QK, qt, zero), jnp.where(row >= A_QK, qt, zero)], axis=1)

    def scores(k):
        return jnp.dot(k, qq, preferred_element_type=F32)

    def twice(b):
        return jnp.concatenate([b, b], axis=1)

    m, l = _first_block(scores(km_ref[...]) + twice(bm_ref[...]), vmt_ref[...], acc_ref)

    lo = jnp.maximum((i - 1) // 2, 0)
    hi = jnp.minimum((i + 1) // 2 + 1, nk)
    for n in range(2):
        kb = jnp.minimum(lo + n, nk - 1)
        c = jnp.where(lo + n < hi, 0.0, NEG)
        band = band_ref[jnp.clip(2 * kb - i + 2, 0, N_BANDS - 1)]
        start = pl.multiple_of(kb * kb_size, kb_size)
        s = scores(k_ref[pl.ds(start, kb_size), :]) + twice(band)
        m, l = _online_block(s, c, vt_ref[kb], m, l, acc_ref)

    c_lo = cfar_ref[2 * h]
    c_hi = cfar_ref[2 * h + 1]

    def c_of(j):
        return jnp.where(j < lo, c_lo, jnp.where(j >= hi, c_hi, NEG))

    m, l = _pipelined_blocks(qq, k_ref, vt_ref, c_of, m, l, acc_ref, s_buf, p_buf)

    lp = lam_ref[...]
    lam = (jnp.exp(jnp.sum(lp[0:1] * lp[1:2], axis=1, keepdims=True))
           - jnp.exp(jnp.sum(lp[2:3] * lp[3:4], axis=1, keepdims=True)) + LAM_INIT)
    r = 1.0 / l
    acc = acc_ref[...]
    o = acc[:, :mq] * r[:, :mq] - lam * (acc[:, mq:] * r[:, mq:])
    o_ref[...] = (_rms(o.T, g_ref[...]) * (1.0 - LAM_INIT)).astype(BF16)


def _attn_scratch(width):
    return [pltpu.VMEM((HEAD, width), F32),
            pltpu.VMEM((2, KEY_BLOCK, width), F32),
            pltpu.VMEM((2, KEY_BLOCK, width), BF16)]


def _attn_a(cfar, lam_p, subln_g, qat, ka, vat, kma, vmat, bias_meta, bands, *, batch):
    n = ka.shape[0]
    lr = n // batch
    nk = lr // KEY_BLOCK
    nq = lr // A_MQ
    return pl.pallas_call(
        _attn_a_kernel,
        grid=(batch, A_HEADS, nq),
        in_specs=[
            pl.BlockSpec(memory_space=pltpu.SMEM),
            pl.BlockSpec((4, A_QK), lambda b, h, i: (0, 0)),
            pl.BlockSpec((1, HEAD), lambda b, h, i: (0, 0)),
            pl.BlockSpec((None, HEAD, A_MQ), lambda b, h, i: (h, 0, b * nq + i)),
            pl.BlockSpec((lr, HEAD), lambda b, h, i: (b, h)),
            pl.BlockSpec((None, nk, HEAD, KEY_BLOCK), lambda b, h, i: (h, b, 0, 0)),
            pl.BlockSpec((META_PAD, HEAD), lambda b, h, i: (0, h)),
            pl.BlockSpec((None, None, HEAD, META_PAD), lambda b, h, i: (h, 0, 0, 0)),
            pl.BlockSpec((None, None, META_PAD, A_MQ), lambda b, h, i: (h, jnp.minimum(i, 1), 0, 0)),
            pl.BlockSpec((None, N_BANDS, KEY_BLOCK, A_MQ), lambda b, h, i: (h, 0, 0, 0)),
        ],
        out_specs=pl.BlockSpec((A_MQ, HEAD), lambda b, h, i: (b * nq + i, h)),
        out_shape=jax.ShapeDtypeStruct((n, A_COLS), BF16),
        scratch_shapes=_attn_scratch(2 * A_MQ),
        compiler_params=_params(3),
        name="attn_a",
    )(cfar, lam_p, subln_g, qat, ka, vat, kma, vmat, bias_meta, bands)


def _attn_b_kernel(qt_ref, k_ref, vt_ref, km_ref, vmt_ref, mask_ref, o_ref, acc_ref, s_buf, p_buf):
    mq = qt_ref.shape[2]
    qq = jnp.concatenate([qt_ref[g] for g in range(B_GROUP)], axis=1)
    s = jnp.dot(km_ref[...], qq, preferred_element_type=F32) + mask_ref[...]
    m, l = _first_block(s, vmt_ref[...], acc_ref)
    m, l = _pipelined_blocks(qq, k_ref, vt_ref, lambda j: 0.0, m, l, acc_ref, s_buf, p_buf)
    o = acc_ref[...] * (1.0 / l)
    for g in range(B_GROUP):
        o_ref[:, g * HEAD:(g + 1) * HEAD] = o[:, g * mq:(g + 1) * mq].T.astype(BF16)


def _attn_b(qbt, kb, vbt, kmb, vmbt, mask, *, batch):
    n = kb.shape[0]
    lr = n // batch
    nk = lr // KEY_BLOCK
    nq = lr // B_MQ
    return pl.pallas_call(
        _attn_b_kernel,
        grid=(batch, B_KV, nq),
        in_specs=[
            pl.BlockSpec((B_GROUP, HEAD, B_MQ), lambda b, n_, i: (n_, 0, b * nq + i)),
            pl.BlockSpec((lr, HEAD), lambda b, n_, i: (b, n_)),
            pl.BlockSpec((None, nk, HEAD, KEY_BLOCK), lambda b, n_, i: (n_, b, 0, 0)),
            pl.BlockSpec((META_PAD, HEAD), lambda b, n_, i: (0, n_)),
            pl.BlockSpec((None, None, HEAD, META_PAD), lambda b, n_, i: (n_, 0, 0, 0)),
            pl.BlockSpec((META_PAD, B_GROUP * B_MQ), lambda b, n_, i: (0, 0)),
        ],
        out_specs=pl.BlockSpec((B_MQ, B_GROUP * HEAD), lambda b, n_, i: (b * nq + i, n_)),
        out_shape=jax.ShapeDtypeStruct((n, B_HEADS * HEAD), BF16),
        scratch_shapes=_attn_scratch(B_GROUP * B_MQ),
        compiler_params=_params(3),
        name="attn_b",
    )(qbt, kb, vbt, kmb, vmbt, mask)


def _out_kernel(h_ref, ya_ref, yb_ref, w_ref, o_ref):
    o_ref[...] = (h_ref[...]
                  + jnp.dot(ya_ref[...], w_ref[:A_COLS, :], preferred_element_type=F32)
                  + jnp.dot(yb_ref[...], w_ref[A_COLS:, :], preferred_element_type=F32))


def _out_proj(h, ya, yb, w_out, *, tm):
    n, d = h.shape
    return pl.pallas_call(
        _out_kernel,
        grid=(n // tm,),
        in_specs=[
            pl.BlockSpec((tm, d), lambda i: (i, 0)),
            pl.BlockSpec((tm, A_COLS), lambda i: (i, 0)),
            pl.BlockSpec((tm, B_HEADS * HEAD), lambda i: (i, 0)),
            pl.BlockSpec(w_out.shape, lambda i: (0, 0), pipeline_mode=pl.Buffered(1)),
        ],
        out_specs=pl.BlockSpec((tm, d), lambda i: (i, 0)),
        out_shape=jax.ShapeDtypeStruct((n, d), F32),
        compiler_params=_params(1),
        name="out_proj",
    )(h, ya, yb, w_out)


def _t5_bucket(rel):
    half = REL_BUCKETS // 2
    max_exact = half // 2
    n = jnp.abs(rel)
    sign_off = jnp.where(rel > 0, half, 0)
    nf = jnp.maximum(n, 1).astype(F32)
    large = max_exact + (jnp.log(nf / max_exact) / math.log(REL_MAX_DIST / max_exact)
                         * (half - max_exact)).astype(jnp.int32)
    large = jnp.minimum(large, half - 1)
    return sign_off + jnp.where(n < max_exact, n, large)


def _toeplitz(vec, rows, cols):
    period = rows + cols - 1
    assert vec.shape[-1] == period
    lead = vec.shape[:-1]
    flat = jnp.tile(vec, (1,) * len(lead) + (rows + 1,))[..., :rows * (period + 1)]
    hankel = flat.reshape(lead + (rows, period + 1))[..., :cols]
    return hankel[..., ::-1]


def _bias_tables(rel_table):
    rel1d = jnp.arange(-REL_CLIP, REL_CLIP + 1)
    t1d = rel_table.astype(F32)[_t5_bucket(rel1d)].T * LOG2E

    def lookup(rel):
        return t1d[:, jnp.clip(rel, -REL_CLIP, REL_CLIP) + REL_CLIP]

    delta = (jnp.arange(N_BANDS) * A_MQ - KEY_BLOCK)[:, None]
    x = jnp.arange(KEY_BLOCK + A_MQ - 1)[None, :]
    bands = _toeplitz(lookup(delta + x - (A_MQ - 1)), KEY_BLOCK, A_MQ)

    c_lo = t1d[:, 0]
    c_hi = t1d[:, -1]
    cfar = jnp.stack([c_lo, c_hi], axis=1).reshape(-1)

    j = jnp.arange(META_PAD)[:, None]
    t = jnp.arange(A_MQ)[None, :]
    first = lookup(j - (N_META + t))
    rest = jnp.broadcast_to(c_lo[:, None, None], first.shape)
    bias_meta = jnp.where((j < N_META)[None], jnp.stack([first, rest], axis=1), NEG)
    return bands, cfar, bias_meta


def _rope_tables(lr):
    rows = lr // GRID_W
    row = jnp.repeat(jnp.arange(rows), GRID_W).astype(F32)
    col = jnp.tile(jnp.arange(GRID_W), rows).astype(F32)
    axis_dim = HEAD // 2
    freqs = ROPE_THETA ** (-jnp.arange(0, axis_dim, 2, dtype=F32) / axis_dim)
    ang = jnp.concatenate([row[:, None] * freqs, col[:, None] * freqs], axis=-1)
    cos, sin = jnp.cos(ang), jnp.sin(ang)
    cosf = jnp.repeat(cos, 2, axis=-1)
    sinf = jnp.stack([-sin, sin], axis=-1).reshape(lr, HEAD)
    return cosf, sinf


def kernel(x_prompt, x_sample, meta_tokens, rel_bias_table, ffn1_norm, ffn1_w_in, ffn1_w_out,
           mix_norm, w_in, diff_lambda, diff_subln, qk_norm, w_out,
           ffn2_norm, ffn2_w_in, ffn2_w_out, final_norm):
    w1i, w1o = ffn1_w_in[0].astype(BF16), ffn1_w_out[0].astype(BF16)
    w2i, w2o = ffn2_w_in[0].astype(BF16), ffn2_w_out[0].astype(BF16)
    wi, wo = w_in[0].astype(BF16), w_out[0].astype(BF16)
    g1, gm, g2 = ffn1_norm, mix_norm, ffn2_norm
    gf = final_norm[None, :]
    qkg = qk_norm[0]
    lam_p = diff_lambda[0]
    subln = diff_subln

    bands, cfar, bias_meta = _bias_tables(rel_bias_table)
    mask_b = jnp.where(jnp.arange(META_PAD)[:, None] < N_META, 0.0, NEG).astype(F32)
    mask_b = jnp.broadcast_to(mask_b, (META_PAD, B_GROUP * B_MQ))

    xm = jnp.zeros((META_PAD, D_MODEL), F32).at[:N_META].set(meta_tokens)
    hm = _ffn(xm, g1, w1i, w1o, gf, tm=META_PAD, final_norm=False)
    ones = jnp.ones((META_PAD, HEAD), F32)
    _, kma, vmat, _, kmb, vmbt = _proj(hm, gm, wi, qkg, ones, jnp.zeros_like(ones), tm=META_PAD)

    def trunk(x):
        batch, lr, d = x.shape
        h = _ffn(x.reshape(batch * lr, d), g1, w1i, w1o, gf, tm=FFN_TM, final_norm=False)
        cosf, sinf = _rope_tables(lr)
        qat, ka, vat, qbt, kb, vbt = _proj(h, gm, wi, qkg, cosf, sinf, tm=PROJ_TM)
        ya = _attn_a(cfar, lam_p, subln, qat, ka, vat, kma, vmat, bias_meta, bands, batch=batch)
        yb = _attn_b(qbt, kb, vbt, kmb, vmbt, mask_b, batch=batch)
        h = _out_proj(h, ya, yb, wo, tm=PROJ_TM)
        y = _ffn(h, g2, w2i, w2o, gf, tm=FFN_TM, final_norm=True)
        return y.reshape(batch, lr, d)

    return trunk(x_prompt), trunk(x_sample)
```

```python
import functools
import math

import jax
import jax.numpy as jnp
from jax import lax
from jax.experimental import pallas as pl
from jax.experimental.pallas import tpu as pltpu

F32 = jnp.float32
BF16 = jnp.bfloat16

EPS = 1e-6
N_META = 16
GRID_W = 64
D_MODEL = 2048
A_HEADS = 8
A_QK = 64
HEAD = 128
B_HEADS = 8
B_KV = 2
B_GROUP = B_HEADS // B_KV
A_COLS = A_HEADS * HEAD
REL_BUCKETS = 32
REL_MAX_DIST = 128
ROPE_THETA = 10000.0
LAM_INIT = 0.8 - 0.6 * math.exp(-0.3 * 0)
LOG2E = math.log2(math.e)
A_QSCALE = A_QK ** -0.5 * LOG2E
B_QSCALE = HEAD ** -0.5 * LOG2E
NEG = -1e30

META_PAD = 128
REL_CLIP = 128

VMEM_LIMIT = 60 * 1024 * 1024

FFN_TM = 1024
FFN_TF = 512
FFN_TN = 512
PROJ_TM = 512
KEY_BLOCK = PROJ_TM
A_MQ = 256
B_MQ = 128
N_BANDS = 4
EXP_HEADROOM = 100.0
BOUND_ROWS = 16
BOUND_SLACK = 1.01


def _rms(x, g):
    return x * lax.rsqrt(jnp.mean(x * x, axis=-1, keepdims=True) + EPS) * g


def _params(n_grid_dims):
    return pltpu.CompilerParams(
        dimension_semantics=("arbitrary",) * n_grid_dims,
        vmem_limit_bytes=VMEM_LIMIT,
    )


def _ffn_kernel(x_ref, g_ref, wg_ref, wu_ref, wo_ref, fg_ref, o_ref, xn_ref, *, final_norm):
    j = pl.program_id(1)

    @pl.when(j == 0)
    def _():
        x = x_ref[...]
        xn_ref[...] = _rms(x, g_ref[...]).astype(BF16)
        o_ref[...] = x

    xn = xn_ref[...]
    gate = jnp.dot(xn, wg_ref[...], preferred_element_type=F32)
    up = jnp.dot(xn, wu_ref[...], preferred_element_type=F32)
    act = (0.5 * gate / (1.0 + jnp.exp(-gate)) * up).astype(BF16)
    d = o_ref.shape[1]
    for n0 in range(0, d, FFN_TN):
        o_ref[:, n0:n0 + FFN_TN] += jnp.dot(act, wo_ref[:, n0:n0 + FFN_TN],
                                            preferred_element_type=F32)

    if final_norm:
        @pl.when(j == pl.num_programs(1) - 1)
        def _():
            o_ref[...] = _rms(o_ref[...], fg_ref[...])


def _ffn(x, norm_g, w_in, w_out, final_g, *, tm, final_norm):
    n, d = x.shape
    dff = w_out.shape[0]
    nj = dff // FFN_TF
    return pl.pallas_call(
        functools.partial(_ffn_kernel, final_norm=final_norm),
        grid=(n // tm, nj),
        in_specs=[
            pl.BlockSpec((tm, d), lambda i, j: (i, 0)),
            pl.BlockSpec((1, d), lambda i, j: (0, 0)),
            pl.BlockSpec((d, FFN_TF), lambda i, j: (0, j)),
            pl.BlockSpec((d, FFN_TF), lambda i, j: (0, j + nj)),
            pl.BlockSpec((FFN_TF, d), lambda i, j: (j, 0)),
            pl.BlockSpec((1, d), lambda i, j: (0, 0)),
        ],
        out_specs=pl.BlockSpec((tm, d), lambda i, j: (i, 0)),
        out_shape=jax.ShapeDtypeStruct((n, d), F32),
        scratch_shapes=[pltpu.VMEM((tm, d), BF16)],
        compiler_params=_params(2),
        name="ffn_final" if final_norm else "ffn",
    )(x, norm_g, w_in, w_in, w_out, final_g)


def _proj_kernel(h_ref, g_ref, w_ref, qkg_ref, cos_ref, sin_ref,
                 qat_ref, ka_ref, vat_ref, qbt_ref, kb_ref, vbt_ref):
    tm = h_ref.shape[0]
    u = _rms(h_ref[...], g_ref[...]).astype(BF16)
    cosf = cos_ref[...]
    sinf = sin_ref[...]
    lane = lax.broadcasted_iota(jnp.int32, (tm, HEAD), 1)
    even = (lane & 1) == 0

    def proj(c0, width):
        return jnp.dot(u, w_ref[:, c0:c0 + width], preferred_element_type=F32)

    def head(y, hh):
        return y[:, hh * HEAD:(hh + 1) * HEAD]

    def norm_rope(y, g):
        y = _rms(y, g)
        swapped = jnp.where(even, pltpu.roll(y, HEAD - 1, 1), pltpu.roll(y, 1, 1))
        return y * cosf + swapped * sinf

    chunk = 4 * HEAD
    for c in range(2):
        y = proj(c * chunk, chunk)
        for hh in range(4):
            qat_ref[c * 4 + hh] = (head(y, hh) * A_QSCALE).T.astype(BF16)
    for c in range(2):
        ka_ref[:, c * chunk:(c + 1) * chunk] = proj(A_COLS + c * chunk, chunk).astype(BF16)
    for c in range(2):
        y = proj(2 * A_COLS + c * chunk, chunk)
        for hh in range(4):
            vat_ref[c * 4 + hh, 0] = head(y, hh).T.astype(BF16)
    gq = qkg_ref[0:1, :]
    gk = qkg_ref[1:2, :]
    for c in range(2):
        y = proj(3 * A_COLS + c * chunk, chunk)
        for hh in range(4):
            qbt_ref[c * 4 + hh] = (norm_rope(head(y, hh), gq) * B_QSCALE).T.astype(BF16)
    y = proj(4 * A_COLS, chunk)
    for n in range(B_KV):
        kb_ref[:, n * HEAD:(n + 1) * HEAD] = norm_rope(head(y, n), gk).astype(BF16)
        vbt_ref[n, 0] = head(y, B_KV + n).T.astype(BF16)


def _proj(h, norm_g, w_in, qk_g, cosf, sinf, *, tm):
    n, d = h.shape
    nt = n // tm
    ntab = cosf.shape[0] // tm
    in_w = w_in.shape[1]
    out_shape = (
        jax.ShapeDtypeStruct((A_HEADS, HEAD, n), BF16),
        jax.ShapeDtypeStruct((n, A_COLS), BF16),
        jax.ShapeDtypeStruct((A_HEADS, nt, HEAD, tm), BF16),
        jax.ShapeDtypeStruct((B_HEADS, HEAD, n), BF16),
        jax.ShapeDtypeStruct((n, B_KV * HEAD), BF16),
        jax.ShapeDtypeStruct((B_KV, nt, HEAD, tm), BF16),
    )
    return pl.pallas_call(
        _proj_kernel,
        grid=(nt,),
        in_specs=[
            pl.BlockSpec((tm, d), lambda i: (i, 0)),
            pl.BlockSpec((1, d), lambda i: (0, 0)),
            pl.BlockSpec((d, in_w), lambda i: (0, 0), pipeline_mode=pl.Buffered(1)),
            pl.BlockSpec((2, HEAD), lambda i: (0, 0)),
            pl.BlockSpec((tm, HEAD), lambda i: (i % ntab, 0)),
            pl.BlockSpec((tm, HEAD), lambda i: (i % ntab, 0)),
        ],
        out_specs=(
            pl.BlockSpec((A_HEADS, HEAD, tm), lambda i: (0, 0, i)),
            pl.BlockSpec((tm, A_COLS), lambda i: (i, 0)),
            pl.BlockSpec((A_HEADS, 1, HEAD, tm), lambda i: (0, i, 0, 0)),
            pl.BlockSpec((B_HEADS, HEAD, tm), lambda i: (0, 0, i)),
            pl.BlockSpec((tm, B_KV * HEAD), lambda i: (i, 0)),
            pl.BlockSpec((B_KV, 1, HEAD, tm), lambda i: (0, i, 0, 0)),
        ),
        out_shape=out_shape,
        compiler_params=_params(1),
        name="mix_proj",
    )(h, norm_g, w_in, qk_g, cosf, sinf)


def _online_block(s, c, vt, m, l, acc_ref):
    m_new = jnp.maximum(m, jnp.max(s, axis=0, keepdims=True) + c)
    alpha = jnp.exp2(m - m_new)
    p = jnp.exp2(s - (m_new - c))
    l_new = alpha * l + jnp.sum(p, axis=0, keepdims=True)
    acc_ref[...] = alpha * acc_ref[...] + jnp.dot(vt, p.astype(BF16), preferred_element_type=F32)
    return m_new, l_new


def _first_block(s, vt, acc_ref):
    m = jnp.max(s, axis=0, keepdims=True)
    p = jnp.exp2(s - m)
    acc_ref[...] = jnp.dot(vt, p.astype(BF16), preferred_element_type=F32)
    return m, jnp.sum(p, axis=0, keepdims=True)


def _fixed_ref_blocks(key_scores, off_of, vt_ref, l, acc_ref, p_buf, lead=None):
    nk = vt_ref.shape[0]
    assert nk % 2 == 0 and nk >= 2

    def value_stage(j, slot):
        acc_ref[...] += jnp.dot(vt_ref[j], p_buf[slot], preferred_element_type=F32)

    def half(slot, s, off, prev, l):
        if prev is not None:
            value_stage(prev, slot)
        p = jnp.exp2(s - off)
        p_buf[slot] = p.astype(BF16)
        return l + jnp.sum(p, axis=0, keepdims=True)

    prev = (None, None)
    if lead is not None:
        for slot, (scores_fn, off, _) in enumerate(lead):
            l = half(slot, scores_fn(), off, None, l)
        prev = tuple(j for _, _, j in lead)
    for slot in range(2):
        l = half(slot, key_scores(slot), off_of(slot), prev[slot], l)

    def pair(g, l):
        j = 2 * g
        for slot in range(2):
            l = half(slot, key_scores(j + slot), off_of(j + slot), j + slot - 2, l)
        return l

    l = lax.fori_loop(1, nk // 2, pair, l)
    value_stage(nk - 2, 0)
    value_stage(nk - 1, 1)
    return l


def _key_absmax(k_ref, kmax_ref):
    def body(kb, kmax):
        start = pl.multiple_of(kb * KEY_BLOCK, KEY_BLOCK)
        blk = jnp.abs(k_ref[pl.ds(start, KEY_BLOCK), :].astype(F32))
        return jnp.maximum(kmax, jnp.max(blk, axis=0, keepdims=True))
    kmax = lax.fori_loop(0, k_ref.shape[0] // KEY_BLOCK, body, jnp.zeros((1, HEAD), F32))
    kmax_ref[...] = jnp.broadcast_to(kmax, kmax_ref.shape).astype(BF16)


def _score_bound(kmax_ref, qq):
    ub = jnp.dot(kmax_ref[...], jnp.abs(qq), preferred_element_type=F32)
    return ub[0:1] * BOUND_SLACK


def _pipelined_blocks(qq, k_ref, vt_ref, c_of, m, l, acc_ref, s_buf, p_buf):
    nk = vt_ref.shape[0]
    kb_size = vt_ref.shape[2]
    assert nk % 2 == 0 and nk >= 2

    def stage_scores(j, slot, m):
        start = pl.multiple_of(j * kb_size, kb_size)
        s = jnp.dot(k_ref[pl.ds(start, kb_size), :], qq, preferred_element_type=F32)
        s_buf[slot] = s
        c = c_of(j)
        m_new = jnp.maximum(m, jnp.max(s, axis=0, keepdims=True) + c)
        return m_new, jnp.exp2(m - m_new), m_new - c

    def stage_exp(slot, off, alpha, l):
        p = jnp.exp2(s_buf[slot] - off)
        p_buf[slot] = p.astype(BF16)
        return alpha * l + jnp.sum(p, axis=0, keepdims=True)

    def stage_values(j, slot, alpha):
        acc_ref[...] = alpha * acc_ref[...] + jnp.dot(vt_ref[j], p_buf[slot],
                                                      preferred_element_type=F32)

    m, a0, o0 = stage_scores(0, 0, m)
    l = stage_exp(0, o0, a0, l)
    m, a1, o1 = stage_scores(1, 1, m)

    def pair(g, carry):
        m, l, a_pp, a_p, o_p = carry
        j = 2 * g
        stage_values(j - 2, 0, a_pp)
        l = stage_exp(1, o_p, a_p, l)
        m, a_j, o_j = stage_scores(j, 0, m)
        stage_values(j - 1, 1, a_p)
        l = stage_exp(0, o_j, a_j, l)
        m, a_j1, o_j1 = stage_scores(j + 1, 1, m)
        return m, l, a_j, a_j1, o_j1

    m, l, a_pp, a_p, o_p = lax.fori_loop(1, nk // 2, pair, (m, l, a0, a1, o1))
    stage_values(nk - 2, 0, a_pp)
    l = stage_exp(1, o_p, a_p, l)
    stage_values(nk - 1, 1, a_p)
    return m, l


def _attn_a_kernel(cfar_ref, lam_ref, g_ref, qt_ref, k_ref, vt_ref, km_ref, vmt_ref, bm_ref,
                   band_ref, o_ref, acc_ref, s_buf, p_buf, kmax_ref):
    h = pl.program_id(1)
    i = pl.program_id(2)
    mq = qt_ref.shape[1]
    nk = vt_ref.shape[0]
    kb_size = vt_ref.shape[2]

    @pl.when(i == 0)
    def _():
        _key_absmax(k_ref, kmax_ref)

    qt = qt_ref[...]
    row = lax.broadcasted_iota(jnp.int32, qt.shape, 0)
    zero = jnp.zeros_like(qt)
    qq = jnp.concatenate([jnp.where(row < A_QK, qt, zero), jnp.where(row >= A_QK, qt, zero)], axis=1)

    def scores(k):
        return jnp.dot(k, qq, preferred_element_type=F32)

    def twice(b):
        return jnp.concatenate([b, b], axis=1)

    m, l = _first_block(scores(km_ref[...]) + twice(bm_ref[...]), vmt_ref[...], acc_ref)

    lo = jnp.maximum((i - 1) // 2, 0)
    hi = jnp.minimum((i + 1) // 2 + 1, nk)

    def key_scores(kb):
        start = pl.multiple_of(kb * kb_size, kb_size)
        return scores(k_ref[pl.ds(start, kb_size), :])

    def band_block(n):
        kb = jnp.minimum(lo + n, nk - 1)
        band_idx = jnp.clip(2 * kb - i + 2, 0, N_BANDS - 1)
        return (kb, lambda: key_scores(kb) + twice(band_ref[band_idx]),
                jnp.where(lo + n < hi, 0.0, NEG))

    c_lo = cfar_ref[3 * h]
    c_hi = cfar_ref[3 * h + 1]
    c_max = cfar_ref[3 * h + 2]

    def c_of(j):
        return jnp.where(j < lo, c_lo, jnp.where(j >= hi, c_hi, NEG))

    def fixed_reference(m, l):
        lead = []
        for n in range(2):
            kb, scores_fn, c = band_block(n)
            lead.append((scores_fn, m - c, kb))
        return _fixed_ref_blocks(key_scores, lambda j: m - c_of(j), vt_ref, l, acc_ref, p_buf, lead)

    def running_reference(m, l):
        for n in range(2):
            kb, scores_fn, c = band_block(n)
            m, l = _online_block(scores_fn(), c, vt_ref[kb], m, l, acc_ref)
        return _pipelined_blocks(qq, k_ref, vt_ref, c_of, m, l, acc_ref, s_buf, p_buf)[1]

    fits = jnp.max(_score_bound(kmax_ref, qq) + c_max - m) <= EXP_HEADROOM
    l = lax.cond(fits, fixed_reference, running_reference, m, l)

    lp = lam_ref[...]
    lam = (jnp.exp(jnp.sum(lp[0:1] * lp[1:2], axis=1, keepdims=True))
           - jnp.exp(jnp.sum(lp[2:3] * lp[3:4], axis=1, keepdims=True)) + LAM_INIT)
    r = 1.0 / l
    acc = acc_ref[...]
    o = acc[:, :mq] * r[:, :mq] - lam * (acc[:, mq:] * r[:, mq:])
    o_ref[...] = (_rms(o.T, g_ref[...]) * (1.0 - LAM_INIT)).astype(BF16)


def _attn_scratch(width):
    return [pltpu.VMEM((HEAD, width), F32),
            pltpu.VMEM((2, KEY_BLOCK, width), F32),
            pltpu.VMEM((2, KEY_BLOCK, width), BF16),
            pltpu.VMEM((BOUND_ROWS, HEAD), BF16)]


def _attn_a(cfar, lam_p, subln_g, qat, ka, vat, kma, vmat, bias_meta, bands, *, batch):
    n = ka.shape[0]
    lr = n // batch
    nk = lr // KEY_BLOCK
    nq = lr // A_MQ
    return pl.pallas_call(
        _attn_a_kernel,
        grid=(batch, A_HEADS, nq),
        in_specs=[
            pl.BlockSpec(memory_space=pltpu.SMEM),
            pl.BlockSpec((4, A_QK), lambda b, h, i: (0, 0)),
            pl.BlockSpec((1, HEAD), lambda b, h, i: (0, 0)),
            pl.BlockSpec((None, HEAD, A_MQ), lambda b, h, i: (h, 0, b * nq + i)),
            pl.BlockSpec((lr, HEAD), lambda b, h, i: (b, h)),
            pl.BlockSpec((None, nk, HEAD, KEY_BLOCK), lambda b, h, i: (h, b, 0, 0)),
            pl.BlockSpec((META_PAD, HEAD), lambda b, h, i: (0, h)),
            pl.BlockSpec((None, None, HEAD, META_PAD), lambda b, h, i: (h, 0, 0, 0)),
            pl.BlockSpec((None, None, META_PAD, A_MQ), lambda b, h, i: (h, jnp.minimum(i, 1), 0, 0)),
            pl.BlockSpec((None, N_BANDS, KEY_BLOCK, A_MQ), lambda b, h, i: (h, 0, 0, 0)),
        ],
        out_specs=pl.BlockSpec((A_MQ, HEAD), lambda b, h, i: (b * nq + i, h)),
        out_shape=jax.ShapeDtypeStruct((n, A_COLS), BF16),
        scratch_shapes=_attn_scratch(2 * A_MQ),
        compiler_params=_params(3),
        name="attn_a",
    )(cfar, lam_p, subln_g, qat, ka, vat, kma, vmat, bias_meta, bands)


def _attn_b_kernel(qt_ref, k_ref, vt_ref, km_ref, vmt_ref, mask_ref, o_ref, acc_ref, s_buf, p_buf,
                   kmax_ref):
    mq = qt_ref.shape[2]
    nk = vt_ref.shape[0]
    kb_size = vt_ref.shape[2]

    @pl.when(pl.program_id(2) == 0)
    def _():
        _key_absmax(k_ref, kmax_ref)

    qq = jnp.concatenate([qt_ref[g] for g in range(B_GROUP)], axis=1)
    s = jnp.dot(km_ref[...], qq, preferred_element_type=F32) + mask_ref[...]
    m, l = _first_block(s, vmt_ref[...], acc_ref)

    def key_scores(j):
        start = pl.multiple_of(j * kb_size, kb_size)
        return jnp.dot(k_ref[pl.ds(start, kb_size), :], qq, preferred_element_type=F32)

    def fixed_reference(m, l):
        return _fixed_ref_blocks(key_scores, lambda j: m, vt_ref, l, acc_ref, p_buf)

    def running_reference(m, l):
        return _pipelined_blocks(qq, k_ref, vt_ref, lambda j: 0.0, m, l, acc_ref, s_buf, p_buf)[1]

    fits = jnp.max(_score_bound(kmax_ref, qq) - m) <= EXP_HEADROOM
    l = lax.cond(fits, fixed_reference, running_reference, m, l)
    o = acc_ref[...] * (1.0 / l)
    for g in range(B_GROUP):
        o_ref[:, g * HEAD:(g + 1) * HEAD] = o[:, g * mq:(g + 1) * mq].T.astype(BF16)


def _attn_b(qbt, kb, vbt, kmb, vmbt, mask, *, batch):
    n = kb.shape[0]
    lr = n // batch
    nk = lr // KEY_BLOCK
    nq = lr // B_MQ
    return pl.pallas_call(
        _attn_b_kernel,
        grid=(batch, B_KV, nq),
        in_specs=[
            pl.BlockSpec((B_GROUP, HEAD, B_MQ), lambda b, n_, i: (n_, 0, b * nq + i)),
            pl.BlockSpec((lr, HEAD), lambda b, n_, i: (b, n_)),
            pl.BlockSpec((None, nk, HEAD, KEY_BLOCK), lambda b, n_, i: (n_, b, 0, 0)),
            pl.BlockSpec((META_PAD, HEAD), lambda b, n_, i: (0, n_)),
            pl.BlockSpec((None, None, HEAD, META_PAD), lambda b, n_, i: (n_, 0, 0, 0)),
            pl.BlockSpec((META_PAD, B_GROUP * B_MQ), lambda b, n_, i: (0, 0)),
        ],
        out_specs=pl.BlockSpec((B_MQ, B_GROUP * HEAD), lambda b, n_, i: (b * nq + i, n_)),
        out_shape=jax.ShapeDtypeStruct((n, B_HEADS * HEAD), BF16),
        scratch_shapes=_attn_scratch(B_GROUP * B_MQ),
        compiler_params=_params(3),
        name="attn_b",
    )(qbt, kb, vbt, kmb, vmbt, mask)


def _out_kernel(h_ref, ya_ref, yb_ref, w_ref, o_ref):
    o_ref[...] = (h_ref[...]
                  + jnp.dot(ya_ref[...], w_ref[:A_COLS, :], preferred_element_type=F32)
                  + jnp.dot(yb_ref[...], w_ref[A_COLS:, :], preferred_element_type=F32))


def _out_proj(h, ya, yb, w_out, *, tm):
    n, d = h.shape
    return pl.pallas_call(
        _out_kernel,
        grid=(n // tm,),
        in_specs=[
            pl.BlockSpec((tm, d), lambda i: (i, 0)),
            pl.BlockSpec((tm, A_COLS), lambda i: (i, 0)),
            pl.BlockSpec((tm, B_HEADS * HEAD), lambda i: (i, 0)),
            pl.BlockSpec(w_out.shape, lambda i: (0, 0), pipeline_mode=pl.Buffered(1)),
        ],
        out_specs=pl.BlockSpec((tm, d), lambda i: (i, 0)),
        out_shape=jax.ShapeDtypeStruct((n, d), F32),
        compiler_params=_params(1),
        name="out_proj",
    )(h, ya, yb, w_out)


def _t5_bucket(rel):
    half = REL_BUCKETS // 2
    max_exact = half // 2
    n = jnp.abs(rel)
    sign_off = jnp.where(rel > 0, half, 0)
    nf = jnp.maximum(n, 1).astype(F32)
    large = max_exact + (jnp.log(nf / max_exact) / math.log(REL_MAX_DIST / max_exact)
                         * (half - max_exact)).astype(jnp.int32)
    large = jnp.minimum(large, half - 1)
    return sign_off + jnp.where(n < max_exact, n, large)


def _toeplitz(vec, rows, cols):
    period = rows + cols - 1
    assert vec.shape[-1] == period
    lead = vec.shape[:-1]
    flat = jnp.tile(vec, (1,) * len(lead) + (rows + 1,))[..., :rows * (period + 1)]
    hankel = flat.reshape(lead + (rows, period + 1))[..., :cols]
    return hankel[..., ::-1]


def _bias_tables(rel_table):
    rel1d = jnp.arange(-REL_CLIP, REL_CLIP + 1)
    t1d = rel_table.astype(F32)[_t5_bucket(rel1d)].T * LOG2E

    def lookup(rel):
        return t1d[:, jnp.clip(rel, -REL_CLIP, REL_CLIP) + REL_CLIP]

    delta = (jnp.arange(N_BANDS) * A_MQ - KEY_BLOCK)[:, None]
    x = jnp.arange(KEY_BLOCK + A_MQ - 1)[None, :]
    bands = _toeplitz(lookup(delta + x - (A_MQ - 1)), KEY_BLOCK, A_MQ)

    c_lo = t1d[:, 0]
    c_hi = t1d[:, -1]
    cfar = jnp.stack([c_lo, c_hi, jnp.max(t1d, axis=1)], axis=1).reshape(-1)

    j = jnp.arange(META_PAD)[:, None]
    t = jnp.arange(A_MQ)[None, :]
    first = lookup(j - (N_META + t))
    rest = jnp.broadcast_to(c_lo[:, None, None], first.shape)
    bias_meta = jnp.where((j < N_META)[None], jnp.stack([first, rest], axis=1), NEG)
    return bands, cfar, bias_meta


def _rope_tables(lr):
    rows = lr // GRID_W
    row = jnp.repeat(jnp.arange(rows), GRID_W).astype(F32)
    col = jnp.tile(jnp.arange(GRID_W), rows).astype(F32)
    axis_dim = HEAD // 2
    freqs = ROPE_THETA ** (-jnp.arange(0, axis_dim, 2, dtype=F32) / axis_dim)
    ang = jnp.concatenate([row[:, None] * freqs, col[:, None] * freqs], axis=-1)
    cos, sin = jnp.cos(ang), jnp.sin(ang)
    cosf = jnp.repeat(cos, 2, axis=-1)
    sinf = jnp.stack([-sin, sin], axis=-1).reshape(lr, HEAD)
    return cosf, sinf


def kernel(x_prompt, x_sample, meta_tokens, rel_bias_table, ffn1_norm, ffn1_w_in, ffn1_w_out,
           mix_norm, w_in, diff_lambda, diff_subln, qk_norm, w_out,
           ffn2_norm, ffn2_w_in, ffn2_w_out, final_norm):
    w1i, w1o = ffn1_w_in[0].astype(BF16), ffn1_w_out[0].astype(BF16)
    w2i, w2o = ffn2_w_in[0].astype(BF16), ffn2_w_out[0].astype(BF16)
    wi, wo = w_in[0].astype(BF16), w_out[0].astype(BF16)
    g1, gm, g2 = ffn1_norm, mix_norm, ffn2_norm
    gf = final_norm[None, :]
    qkg = qk_norm[0]
    lam_p = diff_lambda[0]
    subln = diff_subln

    bands, cfar, bias_meta = _bias_tables(rel_bias_table)
    mask_b = jnp.where(jnp.arange(META_PAD)[:, None] < N_META, 0.0, NEG).astype(F32)
    mask_b = jnp.broadcast_to(mask_b, (META_PAD, B_GROUP * B_MQ))

    xm = jnp.zeros((META_PAD, D_MODEL), F32).at[:N_META].set(meta_tokens)
    hm = _ffn(xm, g1, w1i, w1o, gf, tm=META_PAD, final_norm=False)
    ones = jnp.ones((META_PAD, HEAD), F32)
    _, kma, vmat, _, kmb, vmbt = _proj(hm, gm, wi, qkg, ones, jnp.zeros_like(ones), tm=META_PAD)

    def trunk(x):
        batch, lr, d = x.shape
        h = _ffn(x.reshape(batch * lr, d), g1, w1i, w1o, gf, tm=FFN_TM, final_norm=False)
        cosf, sinf = _rope_tables(lr)
        qat, ka, vat, qbt, kb, vbt = _proj(h, gm, wi, qkg, cosf, sinf, tm=PROJ_TM)
        ya = _attn_a(cfar, lam_p, subln, qat, ka, vat, kma, vmat, bias_meta, bands, batch=batch)
        yb = _attn_b(qbt, kb, vbt, kmb, vmbt, mask_b, batch=batch)
        h = _out_proj(h, ya, yb, wo, tm=PROJ_TM)
        y = _ffn(h, g2, w2i, w2o, gf, tm=FFN_TM, final_norm=True)
        return y.reshape(batch, lr, d)

    return trunk(x_prompt), trunk(x_sample)
```

```python
import functools
import math

import jax
import jax.numpy as jnp
from jax import lax
from jax.experimental import pallas as pl
from jax.experimental.pallas import tpu as pltpu

F32 = jnp.float32
BF16 = jnp.bfloat16

EPS = 1e-6
N_META = 16
GRID_W = 64
D_MODEL = 2048
A_HEADS = 8
A_QK = 64
HEAD = 128
B_HEADS = 8
B_KV = 2
B_GROUP = B_HEADS // B_KV
A_COLS = A_HEADS * HEAD
REL_BUCKETS = 32
REL_MAX_DIST = 128
ROPE_THETA = 10000.0
LAM_INIT = 0.8 - 0.6 * math.exp(-0.3 * 0)
LOG2E = math.log2(math.e)
A_QSCALE = A_QK ** -0.5 * LOG2E
B_QSCALE = HEAD ** -0.5 * LOG2E
NEG = -1e30

META_PAD = 128
REL_CLIP = 128

VMEM_LIMIT = 60 * 1024 * 1024

FFN_TM = 1024
FFN_TF = 512
FFN_TN = 512
PROJ_TM = 512
KEY_BLOCK = PROJ_TM
A_MQ = 256
B_MQ = 128
N_BANDS = 4
EXP_HEADROOM = 100.0
BOUND_ROWS = 16
BOUND_SLACK = 1.01
FIXED_REF_GROUP = 8
MAX_LEAD = 2


def _rms(x, g):
    return x * lax.rsqrt(jnp.mean(x * x, axis=-1, keepdims=True) + EPS) * g


def _params(n_grid_dims):
    return pltpu.CompilerParams(
        dimension_semantics=("arbitrary",) * n_grid_dims,
        vmem_limit_bytes=VMEM_LIMIT,
    )


def _ffn_kernel(x_ref, g_ref, wg_ref, wu_ref, wo_ref, fg_ref, o_ref, xn_ref, *, final_norm):
    j = pl.program_id(1)

    @pl.when(j == 0)
    def _():
        x = x_ref[...]
        xn_ref[...] = _rms(x, g_ref[...]).astype(BF16)
        o_ref[...] = x

    xn = xn_ref[...]
    gate = jnp.dot(xn, wg_ref[...], preferred_element_type=F32)
    up = jnp.dot(xn, wu_ref[...], preferred_element_type=F32)
    act = (0.5 * gate / (1.0 + jnp.exp(-gate)) * up).astype(BF16)
    d = o_ref.shape[1]
    for n0 in range(0, d, FFN_TN):
        o_ref[:, n0:n0 + FFN_TN] += jnp.dot(act, wo_ref[:, n0:n0 + FFN_TN],
                                            preferred_element_type=F32)

    if final_norm:
        @pl.when(j == pl.num_programs(1) - 1)
        def _():
            o_ref[...] = _rms(o_ref[...], fg_ref[...])


def _ffn(x, norm_g, w_in, w_out, final_g, *, tm, final_norm):
    n, d = x.shape
    dff = w_out.shape[0]
    nj = dff // FFN_TF
    return pl.pallas_call(
        functools.partial(_ffn_kernel, final_norm=final_norm),
        grid=(n // tm, nj),
        in_specs=[
            pl.BlockSpec((tm, d), lambda i, j: (i, 0)),
            pl.BlockSpec((1, d), lambda i, j: (0, 0)),
            pl.BlockSpec((d, FFN_TF), lambda i, j: (0, j)),
            pl.BlockSpec((d, FFN_TF), lambda i, j: (0, j + nj)),
            pl.BlockSpec((FFN_TF, d), lambda i, j: (j, 0)),
            pl.BlockSpec((1, d), lambda i, j: (0, 0)),
        ],
        out_specs=pl.BlockSpec((tm, d), lambda i, j: (i, 0)),
        out_shape=jax.ShapeDtypeStruct((n, d), F32),
        scratch_shapes=[pltpu.VMEM((tm, d), BF16)],
        compiler_params=_params(2),
        name="ffn_final" if final_norm else "ffn",
    )(x, norm_g, w_in, w_in, w_out, final_g)


def _proj_kernel(h_ref, g_ref, w_ref, qkg_ref, cos_ref, sin_ref,
                 qat_ref, ka_ref, vat_ref, qbt_ref, kb_ref, vbt_ref):
    tm = h_ref.shape[0]
    u = _rms(h_ref[...], g_ref[...]).astype(BF16)
    cosf = cos_ref[...]
    sinf = sin_ref[...]
    lane = lax.broadcasted_iota(jnp.int32, (tm, HEAD), 1)
    even = (lane & 1) == 0

    def proj(c0, width):
        return jnp.dot(u, w_ref[:, c0:c0 + width], preferred_element_type=F32)

    def head(y, hh):
        return y[:, hh * HEAD:(hh + 1) * HEAD]

    def norm_rope(y, g):
        y = _rms(y, g)
        swapped = jnp.where(even, pltpu.roll(y, HEAD - 1, 1), pltpu.roll(y, 1, 1))
        return y * cosf + swapped * sinf

    chunk = 4 * HEAD
    for c in range(2):
        y = proj(c * chunk, chunk)
        for hh in range(4):
            qat_ref[c * 4 + hh] = (head(y, hh) * A_QSCALE).T.astype(BF16)
    for c in range(2):
        ka_ref[:, c * chunk:(c + 1) * chunk] = proj(A_COLS + c * chunk, chunk).astype(BF16)
    for c in range(2):
        y = proj(2 * A_COLS + c * chunk, chunk)
        for hh in range(4):
            vat_ref[c * 4 + hh, 0] = head(y, hh).T.astype(BF16)
    gq = qkg_ref[0:1, :]
    gk = qkg_ref[1:2, :]
    for c in range(2):
        y = proj(3 * A_COLS + c * chunk, chunk)
        for hh in range(4):
            qbt_ref[c * 4 + hh] = (norm_rope(head(y, hh), gq) * B_QSCALE).T.astype(BF16)
    y = proj(4 * A_COLS, chunk)
    for n in range(B_KV):
        kb_ref[:, n * HEAD:(n + 1) * HEAD] = norm_rope(head(y, n), gk).astype(BF16)
        vbt_ref[n, 0] = head(y, B_KV + n).T.astype(BF16)


def _proj(h, norm_g, w_in, qk_g, cosf, sinf, *, tm):
    n, d = h.shape
    nt = n // tm
    ntab = cosf.shape[0] // tm
    in_w = w_in.shape[1]
    out_shape = (
        jax.ShapeDtypeStruct((A_HEADS, HEAD, n), BF16),
        jax.ShapeDtypeStruct((n, A_COLS), BF16),
        jax.ShapeDtypeStruct((A_HEADS, nt, HEAD, tm), BF16),
        jax.ShapeDtypeStruct((B_HEADS, HEAD, n), BF16),
        jax.ShapeDtypeStruct((n, B_KV * HEAD), BF16),
        jax.ShapeDtypeStruct((B_KV, nt, HEAD, tm), BF16),
    )
    return pl.pallas_call(
        _proj_kernel,
        grid=(nt,),
        in_specs=[
            pl.BlockSpec((tm, d), lambda i: (i, 0)),
            pl.BlockSpec((1, d), lambda i: (0, 0)),
            pl.BlockSpec((d, in_w), lambda i: (0, 0), pipeline_mode=pl.Buffered(1)),
            pl.BlockSpec((2, HEAD), lambda i: (0, 0)),
            pl.BlockSpec((tm, HEAD), lambda i: (i % ntab, 0)),
            pl.BlockSpec((tm, HEAD), lambda i: (i % ntab, 0)),
        ],
        out_specs=(
            pl.BlockSpec((A_HEADS, HEAD, tm), lambda i: (0, 0, i)),
            pl.BlockSpec((tm, A_COLS), lambda i: (i, 0)),
            pl.BlockSpec((A_HEADS, 1, HEAD, tm), lambda i: (0, i, 0, 0)),
            pl.BlockSpec((B_HEADS, HEAD, tm), lambda i: (0, 0, i)),
            pl.BlockSpec((tm, B_KV * HEAD), lambda i: (i, 0)),
            pl.BlockSpec((B_KV, 1, HEAD, tm), lambda i: (0, i, 0, 0)),
        ),
        out_shape=out_shape,
        compiler_params=_params(1),
        name="mix_proj",
    )(h, norm_g, w_in, qk_g, cosf, sinf)


def _online_block(s, c, vt, m, l, acc_ref):
    m_new = jnp.maximum(m, jnp.max(s, axis=0, keepdims=True) + c)
    alpha = jnp.exp2(m - m_new)
    p = jnp.exp2(s - (m_new - c))
    l_new = alpha * l + jnp.sum(p, axis=0, keepdims=True)
    acc_ref[...] = alpha * acc_ref[...] + jnp.dot(vt, p.astype(BF16), preferred_element_type=F32)
    return m_new, l_new


def _first_block(s, vt, acc_ref):
    m = jnp.max(s, axis=0, keepdims=True)
    p = jnp.exp2(s - m)
    acc_ref[...] = jnp.dot(vt, p.astype(BF16), preferred_element_type=F32)
    return m, jnp.sum(p, axis=0, keepdims=True)


def _fixed_ref_blocks(key_scores, off_of, vt_ref, l, acc_ref, p_buf, lead=None):
    nk = vt_ref.shape[0]
    lead = lead or []
    group = p_buf.shape[0] - MAX_LEAD
    assert nk % group == 0 and len(lead) <= MAX_LEAD

    def value_stage(j, slot):
        acc_ref[...] += jnp.dot(vt_ref[j], p_buf[slot], preferred_element_type=F32)

    def exp_stage(slot, s, off, l):
        p = jnp.exp2(s - off)
        p_buf[slot] = p.astype(BF16)
        return l + jnp.sum(p, axis=0, keepdims=True)

    for n, (scores_fn, off, _) in enumerate(lead):
        l = exp_stage(group + n, scores_fn(), off, l)

    def blocks(j0, prev_of, l):
        for slot in range(group):
            s = key_scores(j0 + slot)
            prev = prev_of(slot)
            if prev is not None:
                value_stage(*prev)
            l = exp_stage(slot, s, off_of(j0 + slot), l)
        return l

    l = blocks(0, lambda slot: (lead[slot][2], group + slot) if slot < len(lead) else None, l)

    def body(g, l):
        j0 = g * group
        return blocks(j0, lambda slot: (j0 - group + slot, slot), l)

    l = lax.fori_loop(1, nk // group, body, l)
    for slot in range(group):
        value_stage(nk - group + slot, slot)
    return l


def _key_absmax(k_ref, kmax_ref):
    def body(kb, kmax):
        start = pl.multiple_of(kb * KEY_BLOCK, KEY_BLOCK)
        blk = jnp.abs(k_ref[pl.ds(start, KEY_BLOCK), :].astype(F32))
        return jnp.maximum(kmax, jnp.max(blk, axis=0, keepdims=True))
    kmax = lax.fori_loop(0, k_ref.shape[0] // KEY_BLOCK, body, jnp.zeros((1, HEAD), F32))
    kmax_ref[...] = jnp.broadcast_to(kmax, kmax_ref.shape).astype(BF16)


def _score_bound(kmax_ref, qq):
    ub = jnp.dot(kmax_ref[...], jnp.abs(qq), preferred_element_type=F32)
    return ub[0:1] * BOUND_SLACK


def _pipelined_blocks(qq, k_ref, vt_ref, c_of, m, l, acc_ref, s_buf, p_buf):
    nk = vt_ref.shape[0]
    kb_size = vt_ref.shape[2]
    assert nk % 2 == 0 and nk >= 2

    def stage_scores(j, slot, m):
        start = pl.multiple_of(j * kb_size, kb_size)
        s = jnp.dot(k_ref[pl.ds(start, kb_size), :], qq, preferred_element_type=F32)
        s_buf[slot] = s
        c = c_of(j)
        m_new = jnp.maximum(m, jnp.max(s, axis=0, keepdims=True) + c)
        return m_new, jnp.exp2(m - m_new), m_new - c

    def stage_exp(slot, off, alpha, l):
        p = jnp.exp2(s_buf[slot] - off)
        p_buf[slot] = p.astype(BF16)
        return alpha * l + jnp.sum(p, axis=0, keepdims=True)

    def stage_values(j, slot, alpha):
        acc_ref[...] = alpha * acc_ref[...] + jnp.dot(vt_ref[j], p_buf[slot],
                                                      preferred_element_type=F32)

    m, a0, o0 = stage_scores(0, 0, m)
    l = stage_exp(0, o0, a0, l)
    m, a1, o1 = stage_scores(1, 1, m)

    def pair(g, carry):
        m, l, a_pp, a_p, o_p = carry
        j = 2 * g
        stage_values(j - 2, 0, a_pp)
        l = stage_exp(1, o_p, a_p, l)
        m, a_j, o_j = stage_scores(j, 0, m)
        stage_values(j - 1, 1, a_p)
        l = stage_exp(0, o_j, a_j, l)
        m, a_j1, o_j1 = stage_scores(j + 1, 1, m)
        return m, l, a_j, a_j1, o_j1

    m, l, a_pp, a_p, o_p = lax.fori_loop(1, nk // 2, pair, (m, l, a0, a1, o1))
    stage_values(nk - 2, 0, a_pp)
    l = stage_exp(1, o_p, a_p, l)
    stage_values(nk - 1, 1, a_p)
    return m, l


def _attn_a_kernel(cfar_ref, lam_ref, g_ref, qt_ref, k_ref, vt_ref, km_ref, vmt_ref, bm_ref,
                   band_ref, o_ref, acc_ref, s_buf, p_buf, kmax_ref):
    h = pl.program_id(1)
    i = pl.program_id(2)
    mq = qt_ref.shape[1]
    nk = vt_ref.shape[0]
    kb_size = vt_ref.shape[2]

    @pl.when(i == 0)
    def _():
        _key_absmax(k_ref, kmax_ref)

    qt = qt_ref[...]
    row = lax.broadcasted_iota(jnp.int32, qt.shape, 0)
    zero = jnp.zeros_like(qt)
    qq = jnp.concatenate([jnp.where(row < A_QK, qt, zero), jnp.where(row >= A_QK, qt, zero)], axis=1)

    def scores(k):
        return jnp.dot(k, qq, preferred_element_type=F32)

    def twice(b):
        return jnp.concatenate([b, b], axis=1)

    m, l = _first_block(scores(km_ref[...]) + twice(bm_ref[...]), vmt_ref[...], acc_ref)

    lo = jnp.maximum((i - 1) // 2, 0)
    hi = jnp.minimum((i + 1) // 2 + 1, nk)

    def key_scores(kb):
        start = pl.multiple_of(kb * kb_size, kb_size)
        return scores(k_ref[pl.ds(start, kb_size), :])

    def band_block(n):
        kb = jnp.minimum(lo + n, nk - 1)
        band_idx = jnp.clip(2 * kb - i + 2, 0, N_BANDS - 1)
        return (kb, lambda: key_scores(kb) + twice(band_ref[band_idx]),
                jnp.where(lo + n < hi, 0.0, NEG))

    c_lo = cfar_ref[3 * h]
    c_hi = cfar_ref[3 * h + 1]
    c_max = cfar_ref[3 * h + 2]

    def c_of(j):
        return jnp.where(j < lo, c_lo, jnp.where(j >= hi, c_hi, NEG))

    def fixed_reference(m, l):
        lead = []
        for n in range(2):
            kb, scores_fn, c = band_block(n)
            lead.append((scores_fn, m - c, kb))
        return _fixed_ref_blocks(key_scores, lambda j: m - c_of(j), vt_ref, l, acc_ref, p_buf, lead)

    def running_reference(m, l):
        for n in range(2):
            kb, scores_fn, c = band_block(n)
            m, l = _online_block(scores_fn(), c, vt_ref[kb], m, l, acc_ref)
        return _pipelined_blocks(qq, k_ref, vt_ref, c_of, m, l, acc_ref, s_buf, p_buf)[1]

    fits = jnp.max(_score_bound(kmax_ref, qq) + c_max - m) <= EXP_HEADROOM
    l = lax.cond(fits, fixed_reference, running_reference, m, l)

    lp = lam_ref[...]
    lam = (jnp.exp(jnp.sum(lp[0:1] * lp[1:2], axis=1, keepdims=True))
           - jnp.exp(jnp.sum(lp[2:3] * lp[3:4], axis=1, keepdims=True)) + LAM_INIT)
    r = 1.0 / l
    acc = acc_ref[...]
    o = acc[:, :mq] * r[:, :mq] - lam * (acc[:, mq:] * r[:, mq:])
    o_ref[...] = (_rms(o.T, g_ref[...]) * (1.0 - LAM_INIT)).astype(BF16)


def _attn_scratch(width, nk):
    group = min(nk, FIXED_REF_GROUP)
    return [pltpu.VMEM((HEAD, width), F32),
            pltpu.VMEM((2, KEY_BLOCK, width), F32),
            pltpu.VMEM((group + MAX_LEAD, KEY_BLOCK, width), BF16),
            pltpu.VMEM((BOUND_ROWS, HEAD), BF16)]


def _attn_a(cfar, lam_p, subln_g, qat, ka, vat, kma, vmat, bias_meta, bands, *, batch):
    n = ka.shape[0]
    lr = n // batch
    nk = lr // KEY_BLOCK
    nq = lr // A_MQ
    return pl.pallas_call(
        _attn_a_kernel,
        grid=(batch, A_HEADS, nq),
        in_specs=[
            pl.BlockSpec(memory_space=pltpu.SMEM),
            pl.BlockSpec((4, A_QK), lambda b, h, i: (0, 0)),
            pl.BlockSpec((1, HEAD), lambda b, h, i: (0, 0)),
            pl.BlockSpec((None, HEAD, A_MQ), lambda b, h, i: (h, 0, b * nq + i)),
            pl.BlockSpec((lr, HEAD), lambda b, h, i: (b, h)),
            pl.BlockSpec((None, nk, HEAD, KEY_BLOCK), lambda b, h, i: (h, b, 0, 0)),
            pl.BlockSpec((META_PAD, HEAD), lambda b, h, i: (0, h)),
            pl.BlockSpec((None, None, HEAD, META_PAD), lambda b, h, i: (h, 0, 0, 0)),
            pl.BlockSpec((None, None, META_PAD, A_MQ), lambda b, h, i: (h, jnp.minimum(i, 1), 0, 0)),
            pl.BlockSpec((None, N_BANDS, KEY_BLOCK, A_MQ), lambda b, h, i: (h, 0, 0, 0)),
        ],
        out_specs=pl.BlockSpec((A_MQ, HEAD), lambda b, h, i: (b * nq + i, h)),
        out_shape=jax.ShapeDtypeStruct((n, A_COLS), BF16),
        scratch_shapes=_attn_scratch(2 * A_MQ, nk),
        compiler_params=_params(3),
        name="attn_a",
    )(cfar, lam_p, subln_g, qat, ka, vat, kma, vmat, bias_meta, bands)


def _attn_b_kernel(qt_ref, k_ref, vt_ref, km_ref, vmt_ref, mask_ref, o_ref, acc_ref, s_buf, p_buf,
                   kmax_ref):
    mq = qt_ref.shape[2]
    nk = vt_ref.shape[0]
    kb_size = vt_ref.shape[2]

    @pl.when(pl.program_id(2) == 0)
    def _():
        _key_absmax(k_ref, kmax_ref)

    qq = jnp.concatenate([qt_ref[g] for g in range(B_GROUP)], axis=1)
    s = jnp.dot(km_ref[...], qq, preferred_element_type=F32) + mask_ref[...]
    m, l = _first_block(s, vmt_ref[...], acc_ref)

    def key_scores(j):
        start = pl.multiple_of(j * kb_size, kb_size)
        return jnp.dot(k_ref[pl.ds(start, kb_size), :], qq, preferred_element_type=F32)

    def fixed_reference(m, l):
        return _fixed_ref_blocks(key_scores, lambda j: m, vt_ref, l, acc_ref, p_buf)

    def running_reference(m, l):
        return _pipelined_blocks(qq, k_ref, vt_ref, lambda j: 0.0, m, l, acc_ref, s_buf, p_buf)[1]

    fits = jnp.max(_score_bound(kmax_ref, qq) - m) <= EXP_HEADROOM
    l = lax.cond(fits, fixed_reference, running_reference, m, l)
    o = acc_ref[...] * (1.0 / l)
    for g in range(B_GROUP):
        o_ref[:, g * HEAD:(g + 1) * HEAD] = o[:, g * mq:(g + 1) * mq].T.astype(BF16)


def _attn_b(qbt, kb, vbt, kmb, vmbt, mask, *, batch):
    n = kb.shape[0]
    lr = n // batch
    nk = lr // KEY_BLOCK
    nq = lr // B_MQ
    return pl.pallas_call(
        _attn_b_kernel,
        grid=(batch, B_KV, nq),
        in_specs=[
            pl.BlockSpec((B_GROUP, HEAD, B_MQ), lambda b, n_, i: (n_, 0, b * nq + i)),
            pl.BlockSpec((lr, HEAD), lambda b, n_, i: (b, n_)),
            pl.BlockSpec((None, nk, HEAD, KEY_BLOCK), lambda b, n_, i: (n_, b, 0, 0)),
            pl.BlockSpec((META_PAD, HEAD), lambda b, n_, i: (0, n_)),
            pl.BlockSpec((None, None, HEAD, META_PAD), lambda b, n_, i: (n_, 0, 0, 0)),
            pl.BlockSpec((META_PAD, B_GROUP * B_MQ), lambda b, n_, i: (0, 0)),
        ],
        out_specs=pl.BlockSpec((B_MQ, B_GROUP * HEAD), lambda b, n_, i: (b * nq + i, n_)),
        out_shape=jax.ShapeDtypeStruct((n, B_HEADS * HEAD), BF16),
        scratch_shapes=_attn_scratch(B_GROUP * B_MQ, nk),
        compiler_params=_params(3),
        name="attn_b",
    )(qbt, kb, vbt, kmb, vmbt, mask)


def _out_kernel(h_ref, ya_ref, yb_ref, w_ref, o_ref):
    o_ref[...] = (h_ref[...]
                  + jnp.dot(ya_ref[...], w_ref[:A_COLS, :], preferred_element_type=F32)
                  + jnp.dot(yb_ref[...], w_ref[A_COLS:, :], preferred_element_type=F32))


def _out_proj(h, ya, yb, w_out, *, tm):
    n, d = h.shape
    return pl.pallas_call(
        _out_kernel,
        grid=(n // tm,),
        in_specs=[
            pl.BlockSpec((tm, d), lambda i: (i, 0)),
            pl.BlockSpec((tm, A_COLS), lambda i: (i, 0)),
            pl.BlockSpec((tm, B_HEADS * HEAD), lambda i: (i, 0)),
            pl.BlockSpec(w_out.shape, lambda i: (0, 0), pipeline_mode=pl.Buffered(1)),
        ],
        out_specs=pl.BlockSpec((tm, d), lambda i: (i, 0)),
        out_shape=jax.ShapeDtypeStruct((n, d), F32),
        compiler_params=_params(1),
        name="out_proj",
    )(h, ya, yb, w_out)


def _t5_bucket(rel):
    half = REL_BUCKETS // 2
    max_exact = half // 2
    n = jnp.abs(rel)
    sign_off = jnp.where(rel > 0, half, 0)
    nf = jnp.maximum(n, 1).astype(F32)
    large = max_exact + (jnp.log(nf / max_exact) / math.log(REL_MAX_DIST / max_exact)
                         * (half - max_exact)).astype(jnp.int32)
    large = jnp.minimum(large, half - 1)
    return sign_off + jnp.where(n < max_exact, n, large)


def _toeplitz(f, rows, cols):
    period = rows + cols - 1
    x = jnp.arange(period)
    vec = f(jnp.where(x < cols, -x, period - x))
    lead = vec.shape[:-1]
    flat = jnp.tile(vec, (1,) * len(lead) + (rows,))[..., :rows * (period - 1)]
    return flat.reshape(lead + (rows, period - 1))[..., :cols]


def _bias_tables(rel_table):
    rel1d = jnp.arange(-REL_CLIP, REL_CLIP + 1)
    t1d = rel_table.astype(F32)[_t5_bucket(rel1d)].T * LOG2E

    def lookup(rel):
        return t1d[:, jnp.clip(rel, -REL_CLIP, REL_CLIP) + REL_CLIP]

    delta = (jnp.arange(N_BANDS) * A_MQ - KEY_BLOCK)[:, None]
    bands = _toeplitz(lambda y: lookup(delta + y[None, :]), KEY_BLOCK, A_MQ)

    c_lo = t1d[:, 0]
    c_hi = t1d[:, -1]
    cfar = jnp.stack([c_lo, c_hi, jnp.max(t1d, axis=1)], axis=1).reshape(-1)

    j = jnp.arange(META_PAD)[:, None]
    t = jnp.arange(A_MQ)[None, :]
    first = lookup(j - (N_META + t))
    rest = jnp.broadcast_to(c_lo[:, None, None], first.shape)
    bias_meta = jnp.where((j < N_META)[None], jnp.stack([first, rest], axis=1), NEG)
    return bands, cfar, bias_meta


def _rope_tables(lr):
    rows = lr // GRID_W
    row = jnp.repeat(jnp.arange(rows), GRID_W).astype(F32)
    col = jnp.tile(jnp.arange(GRID_W), rows).astype(F32)
    axis_dim = HEAD // 2
    freqs = ROPE_THETA ** (-jnp.arange(0, axis_dim, 2, dtype=F32) / axis_dim)
    ang = jnp.concatenate([row[:, None] * freqs, col[:, None] * freqs], axis=-1)
    cos, sin = jnp.cos(ang), jnp.sin(ang)
    cosf = jnp.repeat(cos, 2, axis=-1)
    sinf = jnp.stack([-sin, sin], axis=-1).reshape(lr, HEAD)
    return cosf, sinf


def kernel(x_prompt, x_sample, meta_tokens, rel_bias_table, ffn1_norm, ffn1_w_in, ffn1_w_out,
           mix_norm, w_in, diff_lambda, diff_subln, qk_norm, w_out,
           ffn2_norm, ffn2_w_in, ffn2_w_out, final_norm):
    w1i, w1o = ffn1_w_in[0].astype(BF16), ffn1_w_out[0].astype(BF16)
    w2i, w2o = ffn2_w_in[0].astype(BF16), ffn2_w_out[0].astype(BF16)
    wi, wo = w_in[0].astype(BF16), w_out[0].astype(BF16)
    g1, gm, g2 = ffn1_norm, mix_norm, ffn2_norm
    gf = final_norm[None, :]
    qkg = qk_norm[0]
    lam_p = diff_lambda[0]
    subln = diff_subln

    bands, cfar, bias_meta = _bias_tables(rel_bias_table)
    mask_b = jnp.where(jnp.arange(META_PAD)[:, None] < N_META, 0.0, NEG).astype(F32)
    mask_b = jnp.broadcast_to(mask_b, (META_PAD, B_GROUP * B_MQ))

    xm = jnp.zeros((META_PAD, D_MODEL), F32).at[:N_META].set(meta_tokens)
    hm = _ffn(xm, g1, w1i, w1o, gf, tm=META_PAD, final_norm=False)
    ones = jnp.ones((META_PAD, HEAD), F32)
    _, kma, vmat, _, kmb, vmbt = _proj(hm, gm, wi, qkg, ones, jnp.zeros_like(ones), tm=META_PAD)

    def trunk(x):
        batch, lr, d = x.shape
        h = _ffn(x.reshape(batch * lr, d), g1, w1i, w1o, gf, tm=FFN_TM, final_norm=False)
        cosf, sinf = _rope_tables(lr)
        qat, ka, vat, qbt, kb, vbt = _proj(h, gm, wi, qkg, cosf, sinf, tm=PROJ_TM)
        ya = _attn_a(cfar, lam_p, subln, qat, ka, vat, kma, vmat, bias_meta, bands, batch=batch)
        yb = _attn_b(qbt, kb, vbt, kmb, vmbt, mask_b, batch=batch)
        h = _out_proj(h, ya, yb, wo, tm=PROJ_TM)
        y = _ffn(h, g2, w2i, w2o, gf, tm=FFN_TM, final_norm=True)
        return y.reshape(batch, lr, d)

    return trunk(x_prompt), trunk(x_sample)
```

```python
import functools
import math

import jax
import jax.numpy as jnp
from jax import lax
from jax.experimental import pallas as pl
from jax.experimental.pallas import tpu as pltpu

F32 = jnp.float32
BF16 = jnp.bfloat16

EPS = 1e-6
N_META = 16
GRID_W = 64
D_MODEL = 2048
A_HEADS = 8
A_QK = 64
HEAD = 128
B_HEADS = 8
B_KV = 2
B_GROUP = B_HEADS // B_KV
A_COLS = A_HEADS * HEAD
REL_BUCKETS = 32
REL_MAX_DIST = 128
ROPE_THETA = 10000.0
LAM_INIT = 0.8 - 0.6 * math.exp(-0.3 * 0)
LOG2E = math.log2(math.e)
A_QSCALE = A_QK ** -0.5 * LOG2E
B_QSCALE = HEAD ** -0.5 * LOG2E
NEG = -1e30

META_PAD = 128
REL_CLIP = 128

VMEM_LIMIT = 60 * 1024 * 1024

FFN_TM = 1024
FFN_TF = 512
FFN_TN = 512
PROJ_TM = 512
KEY_BLOCK = PROJ_TM
A_MQ = 256
B_MQ = 128
N_BANDS = 4
MAX_EXPONENT = 126.0
BOUND_ROWS = 16
BOUND_SLACK = 1.01
FIXED_REF_GROUP = 8


def _rms(x, g):
    return x * lax.rsqrt(jnp.mean(x * x, axis=-1, keepdims=True) + EPS) * g


def _params(n_grid_dims):
    return pltpu.CompilerParams(
        dimension_semantics=("arbitrary",) * n_grid_dims,
        vmem_limit_bytes=VMEM_LIMIT,
    )


def _ffn_kernel(x_ref, g_ref, wg_ref, wu_ref, wo_ref, fg_ref, o_ref, xn_ref, *, final_norm):
    j = pl.program_id(1)

    @pl.when(j == 0)
    def _():
        x = x_ref[...]
        xn_ref[...] = _rms(x, g_ref[...]).astype(BF16)
        o_ref[...] = x

    xn = xn_ref[...]
    gate = jnp.dot(xn, wg_ref[...], preferred_element_type=F32)
    up = jnp.dot(xn, wu_ref[...], preferred_element_type=F32)
    act = (0.5 * gate / (1.0 + jnp.exp(-gate)) * up).astype(BF16)
    d = o_ref.shape[1]
    for n0 in range(0, d, FFN_TN):
        o_ref[:, n0:n0 + FFN_TN] += jnp.dot(act, wo_ref[:, n0:n0 + FFN_TN],
                                            preferred_element_type=F32)

    if final_norm:
        @pl.when(j == pl.num_programs(1) - 1)
        def _():
            o_ref[...] = _rms(o_ref[...], fg_ref[...])


def _ffn(x, norm_g, w_in, w_out, final_g, *, tm, final_norm):
    n, d = x.shape
    dff = w_out.shape[0]
    nj = dff // FFN_TF
    return pl.pallas_call(
        functools.partial(_ffn_kernel, final_norm=final_norm),
        grid=(n // tm, nj),
        in_specs=[
            pl.BlockSpec((tm, d), lambda i, j: (i, 0)),
            pl.BlockSpec((1, d), lambda i, j: (0, 0)),
            pl.BlockSpec((d, FFN_TF), lambda i, j: (0, j)),
            pl.BlockSpec((d, FFN_TF), lambda i, j: (0, j + nj)),
            pl.BlockSpec((FFN_TF, d), lambda i, j: (j, 0)),
            pl.BlockSpec((1, d), lambda i, j: (0, 0)),
        ],
        out_specs=pl.BlockSpec((tm, d), lambda i, j: (i, 0)),
        out_shape=jax.ShapeDtypeStruct((n, d), F32),
        scratch_shapes=[pltpu.VMEM((tm, d), BF16)],
        compiler_params=_params(2),
        name="ffn_final" if final_norm else "ffn",
    )(x, norm_g, w_in, w_in, w_out, final_g)


def _proj_kernel(h_ref, g_ref, w_ref, qkg_ref, cos_ref, sin_ref,
                 qat_ref, ka_ref, vat_ref, qbt_ref, kb_ref, vbt_ref):
    tm = h_ref.shape[0]
    u = _rms(h_ref[...], g_ref[...]).astype(BF16)
    cosf = cos_ref[...]
    sinf = sin_ref[...]
    lane = lax.broadcasted_iota(jnp.int32, (tm, HEAD), 1)
    even = (lane & 1) == 0

    def proj(c0, width):
        return jnp.dot(u, w_ref[:, c0:c0 + width], preferred_element_type=F32)

    def head(y, hh):
        return y[:, hh * HEAD:(hh + 1) * HEAD]

    def norm_rope(y, g):
        y = _rms(y, g)
        swapped = jnp.where(even, pltpu.roll(y, HEAD - 1, 1), pltpu.roll(y, 1, 1))
        return y * cosf + swapped * sinf

    chunk = 4 * HEAD
    for c in range(2):
        y = proj(c * chunk, chunk)
        for hh in range(4):
            qat_ref[c * 4 + hh] = (head(y, hh) * A_QSCALE).T.astype(BF16)
    for c in range(2):
        ka_ref[:, c * chunk:(c + 1) * chunk] = proj(A_COLS + c * chunk, chunk).astype(BF16)
    for c in range(2):
        y = proj(2 * A_COLS + c * chunk, chunk)
        for hh in range(4):
            vat_ref[c * 4 + hh, 0] = head(y, hh).T.astype(BF16)
    gq = qkg_ref[0:1, :]
    gk = qkg_ref[1:2, :]
    for c in range(2):
        y = proj(3 * A_COLS + c * chunk, chunk)
        for hh in range(4):
            qbt_ref[c * 4 + hh] = (norm_rope(head(y, hh), gq) * B_QSCALE).T.astype(BF16)
    y = proj(4 * A_COLS, chunk)
    for n in range(B_KV):
        kb_ref[:, n * HEAD:(n + 1) * HEAD] = norm_rope(head(y, n), gk).astype(BF16)
        vbt_ref[n, 0] = head(y, B_KV + n).T.astype(BF16)


def _proj(h, norm_g, w_in, qk_g, cosf, sinf, *, tm):
    n, d = h.shape
    nt = n // tm
    ntab = cosf.shape[0] // tm
    in_w = w_in.shape[1]
    out_shape = (
        jax.ShapeDtypeStruct((A_HEADS, HEAD, n), BF16),
        jax.ShapeDtypeStruct((n, A_COLS), BF16),
        jax.ShapeDtypeStruct((A_HEADS, nt, HEAD, tm), BF16),
        jax.ShapeDtypeStruct((B_HEADS, HEAD, n), BF16),
        jax.ShapeDtypeStruct((n, B_KV * HEAD), BF16),
        jax.ShapeDtypeStruct((B_KV, nt, HEAD, tm), BF16),
    )
    return pl.pallas_call(
        _proj_kernel,
        grid=(nt,),
        in_specs=[
            pl.BlockSpec((tm, d), lambda i: (i, 0)),
            pl.BlockSpec((1, d), lambda i: (0, 0)),
            pl.BlockSpec((d, in_w), lambda i: (0, 0), pipeline_mode=pl.Buffered(1)),
            pl.BlockSpec((2, HEAD), lambda i: (0, 0)),
            pl.BlockSpec((tm, HEAD), lambda i: (i % ntab, 0)),
            pl.BlockSpec((tm, HEAD), lambda i: (i % ntab, 0)),
        ],
        out_specs=(
            pl.BlockSpec((A_HEADS, HEAD, tm), lambda i: (0, 0, i)),
            pl.BlockSpec((tm, A_COLS), lambda i: (i, 0)),
            pl.BlockSpec((A_HEADS, 1, HEAD, tm), lambda i: (0, i, 0, 0)),
            pl.BlockSpec((B_HEADS, HEAD, tm), lambda i: (0, 0, i)),
            pl.BlockSpec((tm, B_KV * HEAD), lambda i: (i, 0)),
            pl.BlockSpec((B_KV, 1, HEAD, tm), lambda i: (0, i, 0, 0)),
        ),
        out_shape=out_shape,
        compiler_params=_params(1),
        name="mix_proj",
    )(h, norm_g, w_in, qk_g, cosf, sinf)


def _online_block(s, c, vt, m, l, acc_ref):
    m_new = jnp.maximum(m, jnp.max(s, axis=0, keepdims=True) + c)
    alpha = jnp.exp2(m - m_new)
    p = jnp.exp2(s - (m_new - c))
    l_new = alpha * l + jnp.sum(p, axis=0, keepdims=True)
    acc_ref[...] = alpha * acc_ref[...] + jnp.dot(vt, p.astype(BF16), preferred_element_type=F32)
    return m_new, l_new


def _first_block(s, vt, acc_ref):
    m = jnp.max(s, axis=0, keepdims=True)
    p = jnp.exp2(s - m)
    acc_ref[...] = jnp.dot(vt, p.astype(BF16), preferred_element_type=F32)
    return m, jnp.sum(p, axis=0, keepdims=True)


def _fixed_ref_blocks(key_scores, off_of, vt_ref, l, acc_ref, p_buf, lead=()):
    nk = vt_ref.shape[0]
    group = min(nk, FIXED_REF_GROUP)
    assert nk % group == 0 and group % 2 == 0 and len(lead) % 2 == 0

    def value_stage(j, slot):
        acc_ref[...] += jnp.dot(vt_ref[j], p_buf[slot], preferred_element_type=F32)

    def block(slot, s, off, prev, l):
        if prev is not None:
            value_stage(prev, 1 - slot)
        p = jnp.exp2(s - off)
        p_buf[slot] = p.astype(BF16)
        return l + jnp.sum(p, axis=0, keepdims=True)

    prev = None
    for n, (scores_fn, off, j) in enumerate(lead):
        l = block(n % 2, scores_fn(), off, prev, l)
        prev = j
    for u in range(group):
        l = block(u % 2, key_scores(u), off_of(u), prev, l)
        prev = u

    def body(g, l):
        j0 = g * group
        for u in range(group):
            l = block(u % 2, key_scores(j0 + u), off_of(j0 + u), j0 + u - 1, l)
        return l

    l = lax.fori_loop(1, nk // group, body, l)
    value_stage(nk - 1, 1)
    return l


def _key_value_bounds(k_ref, vt_ref, kmax_ref, room_ref):
    nk = vt_ref.shape[0]

    def kbody(kb, kmax):
        start = pl.multiple_of(kb * KEY_BLOCK, KEY_BLOCK)
        blk = jnp.abs(k_ref[pl.ds(start, KEY_BLOCK), :].astype(F32))
        return jnp.maximum(kmax, jnp.max(blk, axis=0, keepdims=True))
    kmax = lax.fori_loop(0, nk, kbody, jnp.zeros((1, HEAD), F32))
    kmax_ref[...] = jnp.broadcast_to(kmax, kmax_ref.shape).astype(BF16)

    def vbody(j, vmax):
        return jnp.maximum(vmax, jnp.max(jnp.abs(vt_ref[j].astype(F32)), axis=0, keepdims=True))
    vmax = lax.fori_loop(0, nk, vbody, jnp.ones((1, KEY_BLOCK), F32))
    vmax = jnp.max(vmax, axis=1, keepdims=True)
    room = MAX_EXPONENT - math.log2(nk * KEY_BLOCK) - jnp.log2(vmax)
    room_ref[...] = jnp.broadcast_to(room, room_ref.shape)


def _fits_fixed_reference(kmax_ref, room_ref, qq, c_max, m):
    ub = jnp.dot(kmax_ref[...], jnp.abs(qq), preferred_element_type=F32)[0:1] * BOUND_SLACK
    return jnp.max(ub + c_max - m - room_ref[0:1, 0:1]) <= 0.0


def _pipelined_blocks(qq, k_ref, vt_ref, c_of, m, l, acc_ref, s_buf, p_buf):
    nk = vt_ref.shape[0]
    kb_size = vt_ref.shape[2]
    assert nk % 2 == 0 and nk >= 2

    def stage_scores(j, slot, m):
        start = pl.multiple_of(j * kb_size, kb_size)
        s = jnp.dot(k_ref[pl.ds(start, kb_size), :], qq, preferred_element_type=F32)
        s_buf[slot] = s
        c = c_of(j)
        m_new = jnp.maximum(m, jnp.max(s, axis=0, keepdims=True) + c)
        return m_new, jnp.exp2(m - m_new), m_new - c

    def stage_exp(slot, off, alpha, l):
        p = jnp.exp2(s_buf[slot] - off)
        p_buf[slot] = p.astype(BF16)
        return alpha * l + jnp.sum(p, axis=0, keepdims=True)

    def stage_values(j, slot, alpha):
        acc_ref[...] = alpha * acc_ref[...] + jnp.dot(vt_ref[j], p_buf[slot],
                                                      preferred_element_type=F32)

    m, a0, o0 = stage_scores(0, 0, m)
    l = stage_exp(0, o0, a0, l)
    m, a1, o1 = stage_scores(1, 1, m)

    def pair(g, carry):
        m, l, a_pp, a_p, o_p = carry
        j = 2 * g
        stage_values(j - 2, 0, a_pp)
        l = stage_exp(1, o_p, a_p, l)
        m, a_j, o_j = stage_scores(j, 0, m)
        stage_values(j - 1, 1, a_p)
        l = stage_exp(0, o_j, a_j, l)
        m, a_j1, o_j1 = stage_scores(j + 1, 1, m)
        return m, l, a_j, a_j1, o_j1

    m, l, a_pp, a_p, o_p = lax.fori_loop(1, nk // 2, pair, (m, l, a0, a1, o1))
    stage_values(nk - 2, 0, a_pp)
    l = stage_exp(1, o_p, a_p, l)
    stage_values(nk - 1, 1, a_p)
    return m, l


def _attn_a_kernel(cfar_ref, lam_ref, g_ref, qt_ref, k_ref, vt_ref, km_ref, vmt_ref, bm_ref,
                   band_ref, o_ref, acc_ref, s_buf, p_buf, kmax_ref, room_ref):
    h = pl.program_id(1)
    i = pl.program_id(2)
    mq = qt_ref.shape[1]
    nk = vt_ref.shape[0]
    kb_size = vt_ref.shape[2]

    @pl.when(i == 0)
    def _():
        _key_value_bounds(k_ref, vt_ref, kmax_ref, room_ref)

    qt = qt_ref[...]
    row = lax.broadcasted_iota(jnp.int32, qt.shape, 0)
    zero = jnp.zeros_like(qt)
    qq = jnp.concatenate([jnp.where(row < A_QK, qt, zero), jnp.where(row >= A_QK, qt, zero)], axis=1)

    def scores(k):
        return jnp.dot(k, qq, preferred_element_type=F32)

    def twice(b):
        return jnp.concatenate([b, b], axis=1)

    m, l = _first_block(scores(km_ref[...]) + twice(bm_ref[...]), vmt_ref[...], acc_ref)

    lo = jnp.maximum((i - 1) // 2, 0)
    hi = jnp.minimum((i + 1) // 2 + 1, nk)

    def key_scores(kb):
        start = pl.multiple_of(kb * kb_size, kb_size)
        return scores(k_ref[pl.ds(start, kb_size), :])

    def band_block(n):
        kb = jnp.minimum(lo + n, nk - 1)
        band_idx = jnp.clip(2 * kb - i + 2, 0, N_BANDS - 1)
        return (kb, lambda: key_scores(kb) + twice(band_ref[band_idx]),
                jnp.where(lo + n < hi, 0.0, NEG))

    c_lo = cfar_ref[3 * h]
    c_hi = cfar_ref[3 * h + 1]
    c_max = cfar_ref[3 * h + 2]

    def c_of(j):
        return jnp.where(j < lo, c_lo, jnp.where(j >= hi, c_hi, NEG))

    def fixed_reference(m, l):
        lead = []
        for n in range(2):
            kb, scores_fn, c = band_block(n)
            lead.append((scores_fn, m - c, kb))
        return _fixed_ref_blocks(key_scores, lambda j: m - c_of(j), vt_ref, l, acc_ref, p_buf, lead)

    def running_reference(m, l):
        for n in range(2):
            kb, scores_fn, c = band_block(n)
            m, l = _online_block(scores_fn(), c, vt_ref[kb], m, l, acc_ref)
        return _pipelined_blocks(qq, k_ref, vt_ref, c_of, m, l, acc_ref, s_buf, p_buf)[1]

    fits = _fits_fixed_reference(kmax_ref, room_ref, qq, c_max, m)
    l = lax.cond(fits, fixed_reference, running_reference, m, l)

    lp = lam_ref[...]
    lam = (jnp.exp(jnp.sum(lp[0:1] * lp[1:2], axis=1, keepdims=True))
           - jnp.exp(jnp.sum(lp[2:3] * lp[3:4], axis=1, keepdims=True)) + LAM_INIT)
    r = 1.0 / l
    acc = acc_ref[...]
    o = acc[:, :mq] * r[:, :mq] - lam * (acc[:, mq:] * r[:, mq:])
    o_ref[...] = (_rms(o.T, g_ref[...]) * (1.0 - LAM_INIT)).astype(BF16)


def _attn_scratch(width):
    return [pltpu.VMEM((HEAD, width), F32),
            pltpu.VMEM((2, KEY_BLOCK, width), F32),
            pltpu.VMEM((2, KEY_BLOCK, width), BF16),
            pltpu.VMEM((BOUND_ROWS, HEAD), BF16),
            pltpu.VMEM((8, HEAD), F32)]


def _attn_a(cfar, lam_p, subln_g, qat, ka, vat, kma, vmat, bias_meta, bands, *, batch):
    n = ka.shape[0]
    lr = n // batch
    nk = lr // KEY_BLOCK
    nq = lr // A_MQ
    return pl.pallas_call(
        _attn_a_kernel,
        grid=(batch, A_HEADS, nq),
        in_specs=[
            pl.BlockSpec(memory_space=pltpu.SMEM),
            pl.BlockSpec((4, A_QK), lambda b, h, i: (0, 0)),
            pl.BlockSpec((1, HEAD), lambda b, h, i: (0, 0)),
            pl.BlockSpec((None, HEAD, A_MQ), lambda b, h, i: (h, 0, b * nq + i)),
            pl.BlockSpec((lr, HEAD), lambda b, h, i: (b, h)),
            pl.BlockSpec((None, nk, HEAD, KEY_BLOCK), lambda b, h, i: (h, b, 0, 0)),
            pl.BlockSpec((META_PAD, HEAD), lambda b, h, i: (0, h)),
            pl.BlockSpec((None, None, HEAD, META_PAD), lambda b, h, i: (h, 0, 0, 0)),
            pl.BlockSpec((None, None, META_PAD, A_MQ), lambda b, h, i: (h, jnp.minimum(i, 1), 0, 0)),
            pl.BlockSpec((None, N_BANDS, KEY_BLOCK, A_MQ), lambda b, h, i: (h, 0, 0, 0)),
        ],
        out_specs=pl.BlockSpec((A_MQ, HEAD), lambda b, h, i: (b * nq + i, h)),
        out_shape=jax.ShapeDtypeStruct((n, A_COLS), BF16),
        scratch_shapes=_attn_scratch(2 * A_MQ),
        compiler_params=_params(3),
        name="attn_a",
    )(cfar, lam_p, subln_g, qat, ka, vat, kma, vmat, bias_meta, bands)


def _attn_b_kernel(qt_ref, k_ref, vt_ref, km_ref, vmt_ref, mask_ref, o_ref, acc_ref, s_buf, p_buf,
                   kmax_ref, room_ref):
    mq = qt_ref.shape[2]
    nk = vt_ref.shape[0]
    kb_size = vt_ref.shape[2]

    @pl.when(pl.program_id(2) == 0)
    def _():
        _key_value_bounds(k_ref, vt_ref, kmax_ref, room_ref)

    qq = jnp.concatenate([qt_ref[g] for g in range(B_GROUP)], axis=1)
    s = jnp.dot(km_ref[...], qq, preferred_element_type=F32) + mask_ref[...]
    m, l = _first_block(s, vmt_ref[...], acc_ref)

    def key_scores(j):
        start = pl.multiple_of(j * kb_size, kb_size)
        return jnp.dot(k_ref[pl.ds(start, kb_size), :], qq, preferred_element_type=F32)

    def fixed_reference(m, l):
        return _fixed_ref_blocks(key_scores, lambda j: m, vt_ref, l, acc_ref, p_buf)

    def running_reference(m, l):
        return _pipelined_blocks(qq, k_ref, vt_ref, lambda j: 0.0, m, l, acc_ref, s_buf, p_buf)[1]

    fits = _fits_fixed_reference(kmax_ref, room_ref, qq, 0.0, m)
    l = lax.cond(fits, fixed_reference, running_reference, m, l)
    o = acc_ref[...] * (1.0 / l)
    for g in range(B_GROUP):
        o_ref[:, g * HEAD:(g + 1) * HEAD] = o[:, g * mq:(g + 1) * mq].T.astype(BF16)


def _attn_b(qbt, kb, vbt, kmb, vmbt, mask, *, batch):
    n = kb.shape[0]
    lr = n // batch
    nk = lr // KEY_BLOCK
    nq = lr // B_MQ
    return pl.pallas_call(
        _attn_b_kernel,
        grid=(batch, B_KV, nq),
        in_specs=[
            pl.BlockSpec((B_GROUP, HEAD, B_MQ), lambda b, n_, i: (n_, 0, b * nq + i)),
            pl.BlockSpec((lr, HEAD), lambda b, n_, i: (b, n_)),
            pl.BlockSpec((None, nk, HEAD, KEY_BLOCK), lambda b, n_, i: (n_, b, 0, 0)),
            pl.BlockSpec((META_PAD, HEAD), lambda b, n_, i: (0, n_)),
            pl.BlockSpec((None, None, HEAD, META_PAD), lambda b, n_, i: (n_, 0, 0, 0)),
            pl.BlockSpec((META_PAD, B_GROUP * B_MQ), lambda b, n_, i: (0, 0)),
        ],
        out_specs=pl.BlockSpec((B_MQ, B_GROUP * HEAD), lambda b, n_, i: (b * nq + i, n_)),
        out_shape=jax.ShapeDtypeStruct((n, B_HEADS * HEAD), BF16),
        scratch_shapes=_attn_scratch(B_GROUP * B_MQ),
        compiler_params=_params(3),
        name="attn_b",
    )(qbt, kb, vbt, kmb, vmbt, mask)


def _out_kernel(h_ref, ya_ref, yb_ref, w_ref, o_ref):
    o_ref[...] = (h_ref[...]
                  + jnp.dot(ya_ref[...], w_ref[:A_COLS, :], preferred_element_type=F32)
                  + jnp.dot(yb_ref[...], w_ref[A_COLS:, :], preferred_element_type=F32))


def _out_proj(h, ya, yb, w_out, *, tm):
    n, d = h.shape
    return pl.pallas_call(
        _out_kernel,
        grid=(n // tm,),
        in_specs=[
            pl.BlockSpec((tm, d), lambda i: (i, 0)),
            pl.BlockSpec((tm, A_COLS), lambda i: (i, 0)),
            pl.BlockSpec((tm, B_HEADS * HEAD), lambda i: (i, 0)),
            pl.BlockSpec(w_out.shape, lambda i: (0, 0), pipeline_mode=pl.Buffered(1)),
        ],
        out_specs=pl.BlockSpec((tm, d), lambda i: (i, 0)),
        out_shape=jax.ShapeDtypeStruct((n, d), F32),
        compiler_params=_params(1),
        name="out_proj",
    )(h, ya, yb, w_out)


def _t5_bucket(rel):
    half = REL_BUCKETS // 2
    max_exact = half // 2
    n = jnp.abs(rel)
    sign_off = jnp.where(rel > 0, half, 0)
    nf = jnp.maximum(n, 1).astype(F32)
    large = max_exact + (jnp.log(nf / max_exact) / math.log(REL_MAX_DIST / max_exact)
                         * (half - max_exact)).astype(jnp.int32)
    large = jnp.minimum(large, half - 1)
    return sign_off + jnp.where(n < max_exact, n, large)


def _toeplitz_kernel(vec_ref, o_ref):
    rows, cols = o_ref.shape
    x = jnp.broadcast_to(vec_ref[...], (rows, vec_ref.shape[1]))
    o_ref[...] = pltpu.roll(x, 0, 1, stride=1, stride_axis=0)[:, :cols]


def _toeplitz(f, rows, cols):
    period = rows + cols
    x = jnp.arange(period)
    vec = f(jnp.where(x < cols, -x, period - x))
    n = vec.shape[0]
    return pl.pallas_call(
        _toeplitz_kernel,
        grid=(n,),
        in_specs=[pl.BlockSpec((None, 1, period), lambda i: (i, 0, 0))],
        out_specs=pl.BlockSpec((None, rows, cols), lambda i: (i, 0, 0)),
        out_shape=jax.ShapeDtypeStruct((n, rows, cols), F32),
        compiler_params=_params(1),
        name="bias_bands",
    )(vec[:, None, :])


def _bias_tables(rel_table):
    rel1d = jnp.arange(-REL_CLIP, REL_CLIP + 1)
    t1d = rel_table.astype(F32)[_t5_bucket(rel1d)].T * LOG2E

    def lookup(rel):
        return t1d[:, jnp.clip(rel, -REL_CLIP, REL_CLIP) + REL_CLIP]

    delta = (jnp.arange(N_BANDS) * A_MQ - KEY_BLOCK)[:, None]
    bands = _toeplitz(lambda y: lookup(delta + y[None, :]).reshape(A_HEADS * N_BANDS, -1),
                      KEY_BLOCK, A_MQ).reshape(A_HEADS, N_BANDS, KEY_BLOCK, A_MQ)

    c_lo = t1d[:, 0]
    c_hi = t1d[:, -1]
    cfar = jnp.stack([c_lo, c_hi, jnp.max(t1d, axis=1)], axis=1).reshape(-1)

    j = jnp.arange(META_PAD)[:, None]
    t = jnp.arange(A_MQ)[None, :]
    first = lookup(j - (N_META + t))
    rest = jnp.broadcast_to(c_lo[:, None, None], first.shape)
    bias_meta = jnp.where((j < N_META)[None], jnp.stack([first, rest], axis=1), NEG)
    return bands, cfar, bias_meta


def _rope_tables(lr):
    rows = lr // GRID_W
    row = jnp.repeat(jnp.arange(rows), GRID_W).astype(F32)
    col = jnp.tile(jnp.arange(GRID_W), rows).astype(F32)
    axis_dim = HEAD // 2
    freqs = ROPE_THETA ** (-jnp.arange(0, axis_dim, 2, dtype=F32) / axis_dim)
    ang = jnp.concatenate([row[:, None] * freqs, col[:, None] * freqs], axis=-1)
    cos, sin = jnp.cos(ang), jnp.sin(ang)
    cosf = jnp.repeat(cos, 2, axis=-1)
    sinf = jnp.stack([-sin, sin], axis=-1).reshape(lr, HEAD)
    return cosf, sinf


def kernel(x_prompt, x_sample, meta_tokens, rel_bias_table, ffn1_norm, ffn1_w_in, ffn1_w_out,
           mix_norm, w_in, diff_lambda, diff_subln, qk_norm, w_out,
           ffn2_norm, ffn2_w_in, ffn2_w_out, final_norm):
    w1i, w1o = ffn1_w_in[0].astype(BF16), ffn1_w_out[0].astype(BF16)
    w2i, w2o = ffn2_w_in[0].astype(BF16), ffn2_w_out[0].astype(BF16)
    wi, wo = w_in[0].astype(BF16), w_out[0].astype(BF16)
    g1, gm, g2 = ffn1_norm, mix_norm, ffn2_norm
    gf = final_norm[None, :]
    qkg = qk_norm[0]
    lam_p = diff_lambda[0]
    subln = diff_subln

    bands, cfar, bias_meta = _bias_tables(rel_bias_table)
    mask_b = jnp.where(jnp.arange(META_PAD)[:, None] < N_META, 0.0, NEG).astype(F32)
    mask_b = jnp.broadcast_to(mask_b, (META_PAD, B_GROUP * B_MQ))

    xm = jnp.zeros((META_PAD, D_MODEL), F32).at[:N_META].set(meta_tokens)
    hm = _ffn(xm, g1, w1i, w1o, gf, tm=META_PAD, final_norm=False)
    ones = jnp.ones((META_PAD, HEAD), F32)
    _, kma, vmat, _, kmb, vmbt = _proj(hm, gm, wi, qkg, ones, jnp.zeros_like(ones), tm=META_PAD)

    def trunk(x):
        batch, lr, d = x.shape
        h = _ffn(x.reshape(batch * lr, d), g1, w1i, w1o, gf, tm=FFN_TM, final_norm=False)
        cosf, sinf = _rope_tables(lr)
        qat, ka, vat, qbt, kb, vbt = _proj(h, gm, wi, qkg, cosf, sinf, tm=PROJ_TM)
        ya = _attn_a(cfar, lam_p, subln, qat, ka, vat, kma, vmat, bias_meta, bands, batch=batch)
        yb = _attn_b(qbt, kb, vbt, kmb, vmbt, mask_b, batch=batch)
        h = _out_proj(h, ya, yb, wo, tm=PROJ_TM)
        y = _ffn(h, g2, w2i, w2o, gf, tm=FFN_TM, final_norm=True)
        return y.reshape(batch, lr, d)

    return trunk(x_prompt), trunk(x_sample)
```

```python
import functools
import math

import jax
import jax.numpy as jnp
from jax import lax
from jax.experimental import pallas as pl
from jax.experimental.pallas import tpu as pltpu

F32 = jnp.float32
BF16 = jnp.bfloat16

EPS = 1e-6
N_META = 16
GRID_W = 64
D_MODEL = 2048
A_HEADS = 8
A_QK = 64
HEAD = 128
B_HEADS = 8
B_KV = 2
B_GROUP = B_HEADS // B_KV
A_COLS = A_HEADS * HEAD
REL_BUCKETS = 32
REL_MAX_DIST = 128
ROPE_THETA = 10000.0
LAM_INIT = 0.8 - 0.6 * math.exp(-0.3 * 0)
LOG2E = math.log2(math.e)
A_QSCALE = A_QK ** -0.5 * LOG2E
B_QSCALE = HEAD ** -0.5 * LOG2E
NEG = -1e30

META_PAD = 128
REL_CLIP = 128

VMEM_LIMIT = 60 * 1024 * 1024

FFN_TM = 1024
FFN_TF = 512
FFN_TN = 512
PROJ_TM = 512
KEY_BLOCK = PROJ_TM
A_MQ = 256
B_MQ = 256
N_BANDS = 6
MAX_EXPONENT = 126.0
BOUND_ROWS = 16
BOUND_SLACK = 1.01
FIXED_REF_GROUP_MAX = 10


def _rms(x, g):
    return x * lax.rsqrt(jnp.mean(x * x, axis=-1, keepdims=True) + EPS) * g


def _params(n_grid_dims):
    return pltpu.CompilerParams(
        dimension_semantics=("arbitrary",) * n_grid_dims,
        vmem_limit_bytes=VMEM_LIMIT,
    )


def _ffn_kernel(x_ref, g_ref, wg_ref, wu_ref, wo_ref, fg_ref, o_ref, xn_ref, *, final_norm):
    j = pl.program_id(1)

    @pl.when(j == 0)
    def _():
        x = x_ref[...]
        xn_ref[...] = _rms(x, g_ref[...]).astype(BF16)
        o_ref[...] = x

    xn = xn_ref[...]
    gate = jnp.dot(xn, wg_ref[...], preferred_element_type=F32)
    up = jnp.dot(xn, wu_ref[...], preferred_element_type=F32)
    act = (0.5 * gate / (1.0 + jnp.exp(-gate)) * up).astype(BF16)
    d = o_ref.shape[1]
    for n0 in range(0, d, FFN_TN):
        o_ref[:, n0:n0 + FFN_TN] += jnp.dot(act, wo_ref[:, n0:n0 + FFN_TN],
                                            preferred_element_type=F32)

    if final_norm:
        @pl.when(j == pl.num_programs(1) - 1)
        def _():
            o_ref[...] = _rms(o_ref[...], fg_ref[...])


def _ffn(x, norm_g, w_in, w_out, final_g, *, tm, final_norm):
    n, d = x.shape
    dff = w_out.shape[0]
    nj = dff // FFN_TF
    return pl.pallas_call(
        functools.partial(_ffn_kernel, final_norm=final_norm),
        grid=(n // tm, nj),
        in_specs=[
            pl.BlockSpec((tm, d), lambda i, j: (i, 0)),
            pl.BlockSpec((1, d), lambda i, j: (0, 0)),
            pl.BlockSpec((d, FFN_TF), lambda i, j: (0, j)),
            pl.BlockSpec((d, FFN_TF), lambda i, j: (0, j + nj)),
            pl.BlockSpec((FFN_TF, d), lambda i, j: (j, 0)),
            pl.BlockSpec((1, d), lambda i, j: (0, 0)),
        ],
        out_specs=pl.BlockSpec((tm, d), lambda i, j: (i, 0)),
        out_shape=jax.ShapeDtypeStruct((n, d), F32),
        scratch_shapes=[pltpu.VMEM((tm, d), BF16)],
        compiler_params=_params(2),
        name="ffn_final" if final_norm else "ffn",
    )(x, norm_g, w_in, w_in, w_out, final_g)


def _proj_kernel(h_ref, g_ref, w_ref, qkg_ref, cos_ref, sin_ref,
                 qat_ref, ka_ref, vat_ref, qbt_ref, kb_ref, vbt_ref):
    tm = h_ref.shape[0]
    u = _rms(h_ref[...], g_ref[...]).astype(BF16)
    cosf = cos_ref[...]
    sinf = sin_ref[...]
    lane = lax.broadcasted_iota(jnp.int32, (tm, HEAD), 1)
    even = (lane & 1) == 0

    def proj(c0, width):
        return jnp.dot(u, w_ref[:, c0:c0 + width], preferred_element_type=F32)

    def head(y, hh):
        return y[:, hh * HEAD:(hh + 1) * HEAD]

    def norm_rope(y, g):
        y = _rms(y, g)
        swapped = jnp.where(even, pltpu.roll(y, HEAD - 1, 1), pltpu.roll(y, 1, 1))
        return y * cosf + swapped * sinf

    chunk = 4 * HEAD
    for c in range(2):
        y = proj(c * chunk, chunk)
        for hh in range(4):
            qat_ref[c * 4 + hh] = (head(y, hh) * A_QSCALE).T.astype(BF16)
    for c in range(2):
        ka_ref[:, c * chunk:(c + 1) * chunk] = proj(A_COLS + c * chunk, chunk).astype(BF16)
    for c in range(2):
        y = proj(2 * A_COLS + c * chunk, chunk)
        for hh in range(4):
            vat_ref[c * 4 + hh, 0] = head(y, hh).T.astype(BF16)
    gq = qkg_ref[0:1, :]
    gk = qkg_ref[1:2, :]
    for c in range(2):
        y = proj(3 * A_COLS + c * chunk, chunk)
        for hh in range(4):
            qbt_ref[c * 4 + hh] = (norm_rope(head(y, hh), gq) * B_QSCALE).T.astype(BF16)
    y = proj(4 * A_COLS, chunk)
    for n in range(B_KV):
        kb_ref[:, n * HEAD:(n + 1) * HEAD] = norm_rope(head(y, n), gk).astype(BF16)
        vbt_ref[n, 0] = head(y, B_KV + n).T.astype(BF16)


def _proj(h, norm_g, w_in, qk_g, cosf, sinf, *, tm):
    n, d = h.shape
    nt = n // tm
    ntab = cosf.shape[0] // tm
    in_w = w_in.shape[1]
    out_shape = (
        jax.ShapeDtypeStruct((A_HEADS, HEAD, n), BF16),
        jax.ShapeDtypeStruct((n, A_COLS), BF16),
        jax.ShapeDtypeStruct((A_HEADS, nt, HEAD, tm), BF16),
        jax.ShapeDtypeStruct((B_HEADS, HEAD, n), BF16),
        jax.ShapeDtypeStruct((n, B_KV * HEAD), BF16),
        jax.ShapeDtypeStruct((B_KV, nt, HEAD, tm), BF16),
    )
    return pl.pallas_call(
        _proj_kernel,
        grid=(nt,),
        in_specs=[
            pl.BlockSpec((tm, d), lambda i: (i, 0)),
            pl.BlockSpec((1, d), lambda i: (0, 0)),
            pl.BlockSpec((d, in_w), lambda i: (0, 0), pipeline_mode=pl.Buffered(1)),
            pl.BlockSpec((2, HEAD), lambda i: (0, 0)),
            pl.BlockSpec((tm, HEAD), lambda i: (i % ntab, 0)),
            pl.BlockSpec((tm, HEAD), lambda i: (i % ntab, 0)),
        ],
        out_specs=(
            pl.BlockSpec((A_HEADS, HEAD, tm), lambda i: (0, 0, i)),
            pl.BlockSpec((tm, A_COLS), lambda i: (i, 0)),
            pl.BlockSpec((A_HEADS, 1, HEAD, tm), lambda i: (0, i, 0, 0)),
            pl.BlockSpec((B_HEADS, HEAD, tm), lambda i: (0, 0, i)),
            pl.BlockSpec((tm, B_KV * HEAD), lambda i: (i, 0)),
            pl.BlockSpec((B_KV, 1, HEAD, tm), lambda i: (0, i, 0, 0)),
        ),
        out_shape=out_shape,
        compiler_params=_params(1),
        name="mix_proj",
    )(h, norm_g, w_in, qk_g, cosf, sinf)


def _online_block(s, c, vt, m, l, acc_ref):
    m_new = jnp.maximum(m, jnp.max(s, axis=0, keepdims=True) + c)
    alpha = jnp.exp2(m - m_new)
    p = jnp.exp2(s - (m_new - c))
    l_new = alpha * l + jnp.sum(p, axis=0, keepdims=True)
    acc_ref[...] = alpha * acc_ref[...] + jnp.dot(vt, p.astype(BF16), preferred_element_type=F32)
    return m_new, l_new


def _first_block(s, vt, acc_ref):
    m = jnp.max(s, axis=0, keepdims=True)
    p = jnp.exp2(s - m)
    acc_ref[...] = jnp.dot(vt, p.astype(BF16), preferred_element_type=F32)
    return m, jnp.sum(p, axis=0, keepdims=True)


def _fixed_ref_blocks(n_blocks, block_of, key_scores, off_of, vt_ref, l, acc_ref, p_buf, lead=()):
    group = max(g for g in range(2, FIXED_REF_GROUP_MAX + 1, 2) if n_blocks % g == 0)
    first_slot = len(lead) % 2

    def value_stage(j, slot):
        acc_ref[...] += jnp.dot(vt_ref[j], p_buf[slot], preferred_element_type=F32)

    def block(slot, s, off, prev, l):
        if prev is not None:
            value_stage(prev, 1 - slot)
        p = jnp.exp2(s - off)
        p_buf[slot] = p.astype(BF16)
        return l + jnp.sum(p, axis=0, keepdims=True)

    prev = None
    for n, (scores_fn, off, j) in enumerate(lead):
        l = block(n % 2, scores_fn(), off, prev, l)
        prev = j
    for u in range(group):
        j = block_of(u)
        l = block((first_slot + u) % 2, key_scores(j), off_of(j), prev, l)
        prev = j

    def body(g, carry):
        l, prev = carry
        for u in range(group):
            j = block_of(g * group + u)
            l = block((first_slot + u) % 2, key_scores(j), off_of(j), prev, l)
            prev = j
        return l, prev

    l, prev = lax.fori_loop(1, n_blocks // group, body, (l, jnp.asarray(prev, jnp.int32)))
    value_stage(prev, (first_slot + group - 1) % 2)
    return l


def _key_value_bounds(k_ref, vt_ref, kmax_ref, room_ref):
    nk = vt_ref.shape[0]

    def kbody(kb, kmax):
        start = pl.multiple_of(kb * KEY_BLOCK, KEY_BLOCK)
        blk = jnp.abs(k_ref[pl.ds(start, KEY_BLOCK), :].astype(F32))
        return jnp.maximum(kmax, jnp.max(blk, axis=0, keepdims=True))
    kmax = lax.fori_loop(0, nk, kbody, jnp.zeros((1, HEAD), F32))
    kmax_ref[...] = jnp.broadcast_to(kmax, kmax_ref.shape).astype(BF16)

    def vbody(j, vmax):
        return jnp.maximum(vmax, jnp.max(jnp.abs(vt_ref[j].astype(F32)), axis=0, keepdims=True))
    vmax = lax.fori_loop(0, nk, vbody, jnp.ones((1, KEY_BLOCK), F32))
    vmax = jnp.max(vmax, axis=1, keepdims=True)
    room = MAX_EXPONENT - math.log2(nk * KEY_BLOCK) - jnp.log2(vmax)
    room_ref[...] = jnp.broadcast_to(room, room_ref.shape)


def _fits_fixed_reference(kmax_ref, room_ref, qq, c_max, m):
    ub = jnp.dot(kmax_ref[...], jnp.abs(qq), preferred_element_type=F32)[0:1] * BOUND_SLACK
    return jnp.max(ub + c_max - m - room_ref[0:1, 0:1]) <= 0.0


def _pipelined_blocks(qq, k_ref, vt_ref, c_of, m, l, acc_ref, s_buf, p_buf):
    nk = vt_ref.shape[0]
    kb_size = vt_ref.shape[2]
    assert nk % 2 == 0 and nk >= 2

    def stage_scores(j, slot, m):
        start = pl.multiple_of(j * kb_size, kb_size)
        s = jnp.dot(k_ref[pl.ds(start, kb_size), :], qq, preferred_element_type=F32)
        s_buf[slot] = s
        c = c_of(j)
        m_new = jnp.maximum(m, jnp.max(s, axis=0, keepdims=True) + c)
        return m_new, jnp.exp2(m - m_new), m_new - c

    def stage_exp(slot, off, alpha, l):
        p = jnp.exp2(s_buf[slot] - off)
        p_buf[slot] = p.astype(BF16)
        return alpha * l + jnp.sum(p, axis=0, keepdims=True)

    def stage_values(j, slot, alpha):
        acc_ref[...] = alpha * acc_ref[...] + jnp.dot(vt_ref[j], p_buf[slot],
                                                      preferred_element_type=F32)

    m, a0, o0 = stage_scores(0, 0, m)
    l = stage_exp(0, o0, a0, l)
    m, a1, o1 = stage_scores(1, 1, m)

    def pair(g, carry):
        m, l, a_pp, a_p, o_p = carry
        j = 2 * g
        stage_values(j - 2, 0, a_pp)
        l = stage_exp(1, o_p, a_p, l)
        m, a_j, o_j = stage_scores(j, 0, m)
        stage_values(j - 1, 1, a_p)
        l = stage_exp(0, o_j, a_j, l)
        m, a_j1, o_j1 = stage_scores(j + 1, 1, m)
        return m, l, a_j, a_j1, o_j1

    m, l, a_pp, a_p, o_p = lax.fori_loop(1, nk // 2, pair, (m, l, a0, a1, o1))
    stage_values(nk - 2, 0, a_pp)
    l = stage_exp(1, o_p, a_p, l)
    stage_values(nk - 1, 1, a_p)
    return m, l


def _attn_a_kernel(cfar_ref, lam_ref, g_ref, qt_ref, k_ref, vt_ref, km_ref, vmt_ref, bm_ref,
                   band_ref, o_ref, acc_ref, s_buf, p_buf, kmax_ref, room_ref):
    h = pl.program_id(1)
    i = pl.program_id(2)
    mq = qt_ref.shape[1]
    nk = vt_ref.shape[0]
    kb_size = vt_ref.shape[2]

    @pl.when(i == 0)
    def _():
        _key_value_bounds(k_ref, vt_ref, kmax_ref, room_ref)

    qt = qt_ref[...]
    row = lax.broadcasted_iota(jnp.int32, qt.shape, 0)
    zero = jnp.zeros_like(qt)
    qq = jnp.concatenate([jnp.where(row < A_QK, qt, zero), jnp.where(row >= A_QK, qt, zero)], axis=1)

    def scores(k):
        return jnp.dot(k, qq, preferred_element_type=F32)

    def twice(b):
        return jnp.concatenate([b, b], axis=1)

    m, l = _first_block(scores(km_ref[...]) + twice(bm_ref[...]), vmt_ref[...], acc_ref)

    near = jnp.clip((i - 1) // 2, 0, nk - 2)

    def key_scores(kb):
        start = pl.multiple_of(kb * kb_size, kb_size)
        return scores(k_ref[pl.ds(start, kb_size), :])

    def band_block(n):
        kb = near + n
        return kb, lambda: key_scores(kb) + twice(band_ref[2 * kb - i + 3])

    c_lo = cfar_ref[3 * h]
    c_hi = cfar_ref[3 * h + 1]
    c_max = cfar_ref[3 * h + 2]

    def fixed_reference(m, l):
        lead = []
        for n in range(2):
            kb, scores_fn = band_block(n)
            lead.append((scores_fn, m, kb))
        return _fixed_ref_blocks(nk - 2, lambda jj: jj + jnp.where(jj >= near, 2, 0), key_scores,
                                 lambda j: m - jnp.where(j < near, c_lo, c_hi),
                                 vt_ref, l, acc_ref, p_buf, lead)

    def running_reference(m, l):
        for n in range(2):
            kb, scores_fn = band_block(n)
            m, l = _online_block(scores_fn(), 0.0, vt_ref[kb], m, l, acc_ref)

        def c_of(j):
            return jnp.where(j < near, c_lo, jnp.where(j >= near + 2, c_hi, NEG))
        return _pipelined_blocks(qq, k_ref, vt_ref, c_of, m, l, acc_ref, s_buf, p_buf)[1]

    fits = _fits_fixed_reference(kmax_ref, room_ref, qq, c_max, m)
    l = lax.cond(fits, fixed_reference, running_reference, m, l)

    lp = lam_ref[...]
    lam = (jnp.exp(jnp.sum(lp[0:1] * lp[1:2], axis=1, keepdims=True))
           - jnp.exp(jnp.sum(lp[2:3] * lp[3:4], axis=1, keepdims=True)) + LAM_INIT)
    r = 1.0 / l
    acc = acc_ref[...]
    o = acc[:, :mq] * r[:, :mq] - lam * (acc[:, mq:] * r[:, mq:])
    o_ref[...] = (_rms(o.T, g_ref[...]) * (1.0 - LAM_INIT)).astype(BF16)


def _attn_scratch(width):
    return [pltpu.VMEM((HEAD, width), F32),
            pltpu.VMEM((2, KEY_BLOCK, width), F32),
            pltpu.VMEM((2, KEY_BLOCK, width), BF16),
            pltpu.VMEM((BOUND_ROWS, HEAD), BF16),
            pltpu.VMEM((8, HEAD), F32)]


def _attn_a(cfar, lam_p, subln_g, qat, ka, vat, kma, vmat, bias_meta, bands, *, batch):
    n = ka.shape[0]
    lr = n // batch
    nk = lr // KEY_BLOCK
    nq = lr // A_MQ
    return pl.pallas_call(
        _attn_a_kernel,
        grid=(batch, A_HEADS, nq),
        in_specs=[
            pl.BlockSpec(memory_space=pltpu.SMEM),
            pl.BlockSpec((4, A_QK), lambda b, h, i: (0, 0)),
            pl.BlockSpec((1, HEAD), lambda b, h, i: (0, 0)),
            pl.BlockSpec((None, HEAD, A_MQ), lambda b, h, i: (h, 0, b * nq + i)),
            pl.BlockSpec((lr, HEAD), lambda b, h, i: (b, h)),
            pl.BlockSpec((None, nk, HEAD, KEY_BLOCK), lambda b, h, i: (h, b, 0, 0)),
            pl.BlockSpec((META_PAD, HEAD), lambda b, h, i: (0, h)),
            pl.BlockSpec((None, None, HEAD, META_PAD), lambda b, h, i: (h, 0, 0, 0)),
            pl.BlockSpec((None, None, META_PAD, A_MQ), lambda b, h, i: (h, jnp.minimum(i, 1), 0, 0)),
            pl.BlockSpec((None, N_BANDS, KEY_BLOCK, A_MQ), lambda b, h, i: (h, 0, 0, 0)),
        ],
        out_specs=pl.BlockSpec((A_MQ, HEAD), lambda b, h, i: (b * nq + i, h)),
        out_shape=jax.ShapeDtypeStruct((n, A_COLS), BF16),
        scratch_shapes=_attn_scratch(2 * A_MQ),
        compiler_params=_params(3),
        name="attn_a",
    )(cfar, lam_p, subln_g, qat, ka, vat, kma, vmat, bias_meta, bands)


def _attn_b_kernel(qt_ref, k_ref, vt_ref, km_ref, vmt_ref, mask_ref, o_ref, acc_ref, s_buf, p_buf,
                   kmax_ref, room_ref):
    mq = qt_ref.shape[2]
    nk = vt_ref.shape[0]
    kb_size = vt_ref.shape[2]

    @pl.when(pl.program_id(2) == 0)
    def _():
        _key_value_bounds(k_ref, vt_ref, kmax_ref, room_ref)

    qq = jnp.concatenate([qt_ref[g] for g in range(B_GROUP)], axis=1)
    s = jnp.dot(km_ref[...], qq, preferred_element_type=F32) + mask_ref[...]
    m, l = _first_block(s, vmt_ref[...], acc_ref)

    def key_scores(j):
        start = pl.multiple_of(j * kb_size, kb_size)
        return jnp.dot(k_ref[pl.ds(start, kb_size), :], qq, preferred_element_type=F32)

    def fixed_reference(m, l):
        return _fixed_ref_blocks(nk, lambda jj: jj, key_scores, lambda j: m, vt_ref, l, acc_ref, p_buf)

    def running_reference(m, l):
        return _pipelined_blocks(qq, k_ref, vt_ref, lambda j: 0.0, m, l, acc_ref, s_buf, p_buf)[1]

    fits = _fits_fixed_reference(kmax_ref, room_ref, qq, 0.0, m)
    l = lax.cond(fits, fixed_reference, running_reference, m, l)
    o = acc_ref[...] * (1.0 / l)
    for g in range(B_GROUP):
        o_ref[:, g * HEAD:(g + 1) * HEAD] = o[:, g * mq:(g + 1) * mq].T.astype(BF16)


def _attn_b(qbt, kb, vbt, kmb, vmbt, mask, *, batch):
    n = kb.shape[0]
    lr = n // batch
    nk = lr // KEY_BLOCK
    nq = lr // B_MQ
    return pl.pallas_call(
        _attn_b_kernel,
        grid=(batch, B_KV, nq),
        in_specs=[
            pl.BlockSpec((B_GROUP, HEAD, B_MQ), lambda b, n_, i: (n_, 0, b * nq + i)),
            pl.BlockSpec((lr, HEAD), lambda b, n_, i: (b, n_)),
            pl.BlockSpec((None, nk, HEAD, KEY_BLOCK), lambda b, n_, i: (n_, b, 0, 0)),
            pl.BlockSpec((META_PAD, HEAD), lambda b, n_, i: (0, n_)),
            pl.BlockSpec((None, None, HEAD, META_PAD), lambda b, n_, i: (n_, 0, 0, 0)),
            pl.BlockSpec((META_PAD, B_GROUP * B_MQ), lambda b, n_, i: (0, 0)),
        ],
        out_specs=pl.BlockSpec((B_MQ, B_GROUP * HEAD), lambda b, n_, i: (b * nq + i, n_)),
        out_shape=jax.ShapeDtypeStruct((n, B_HEADS * HEAD), BF16),
        scratch_shapes=_attn_scratch(B_GROUP * B_MQ),
        compiler_params=_params(3),
        name="attn_b",
    )(qbt, kb, vbt, kmb, vmbt, mask)


def _out_kernel(h_ref, ya_ref, yb_ref, w_ref, o_ref):
    o_ref[...] = (h_ref[...]
                  + jnp.dot(ya_ref[...], w_ref[:A_COLS, :], preferred_element_type=F32)
                  + jnp.dot(yb_ref[...], w_ref[A_COLS:, :], preferred_element_type=F32))


def _out_proj(h, ya, yb, w_out, *, tm):
    n, d = h.shape
    return pl.pallas_call(
        _out_kernel,
        grid=(n // tm,),
        in_specs=[
            pl.BlockSpec((tm, d), lambda i: (i, 0)),
            pl.BlockSpec((tm, A_COLS), lambda i: (i, 0)),
            pl.BlockSpec((tm, B_HEADS * HEAD), lambda i: (i, 0)),
            pl.BlockSpec(w_out.shape, lambda i: (0, 0), pipeline_mode=pl.Buffered(1)),
        ],
        out_specs=pl.BlockSpec((tm, d), lambda i: (i, 0)),
        out_shape=jax.ShapeDtypeStruct((n, d), F32),
        compiler_params=_params(1),
        name="out_proj",
    )(h, ya, yb, w_out)


def _t5_bucket(rel):
    half = REL_BUCKETS // 2
    max_exact = half // 2
    n = jnp.abs(rel)
    sign_off = jnp.where(rel > 0, half, 0)
    nf = jnp.maximum(n, 1).astype(F32)
    large = max_exact + (jnp.log(nf / max_exact) / math.log(REL_MAX_DIST / max_exact)
                         * (half - max_exact)).astype(jnp.int32)
    large = jnp.minimum(large, half - 1)
    return sign_off + jnp.where(n < max_exact, n, large)


def _toeplitz_kernel(vec_ref, o_ref):
    rows, cols = o_ref.shape
    x = jnp.broadcast_to(vec_ref[...], (rows, vec_ref.shape[1]))
    o_ref[...] = pltpu.roll(x, 0, 1, stride=1, stride_axis=0)[:, :cols]


def _toeplitz(f, rows, cols):
    period = rows + cols
    x = jnp.arange(period)
    vec = f(jnp.where(x < cols, -x, period - x))
    n = vec.shape[0]
    return pl.pallas_call(
        _toeplitz_kernel,
        grid=(n,),
        in_specs=[pl.BlockSpec((None, 1, period), lambda i: (i, 0, 0))],
        out_specs=pl.BlockSpec((None, rows, cols), lambda i: (i, 0, 0)),
        out_shape=jax.ShapeDtypeStruct((n, rows, cols), F32),
        compiler_params=_params(1),
        name="bias_bands",
    )(vec[:, None, :])


def _bias_tables(rel_table):
    rel1d = jnp.arange(-REL_CLIP, REL_CLIP + 1)
    t1d = rel_table.astype(F32)[_t5_bucket(rel1d)].T * LOG2E

    def lookup(rel):
        return t1d[:, jnp.clip(rel, -REL_CLIP, REL_CLIP) + REL_CLIP]

    delta = ((jnp.arange(N_BANDS) - 1) * A_MQ - KEY_BLOCK)[:, None]
    bands = _toeplitz(lambda y: lookup(delta + y[None, :]).reshape(A_HEADS * N_BANDS, -1),
                      KEY_BLOCK, A_MQ).reshape(A_HEADS, N_BANDS, KEY_BLOCK, A_MQ)

    c_lo = t1d[:, 0]
    c_hi = t1d[:, -1]
    cfar = jnp.stack([c_lo, c_hi, jnp.max(t1d, axis=1)], axis=1).reshape(-1)

    j = jnp.arange(META_PAD)[:, None]
    first = _toeplitz(lambda y: lookup(y - N_META), META_PAD, A_MQ)
    rest = jnp.broadcast_to(c_lo[:, None, None], first.shape)
    bias_meta = jnp.where((j < N_META)[None], jnp.stack([first, rest], axis=1), NEG)
    return bands, cfar, bias_meta


def _rope_tables(lr):
    rows = lr // GRID_W
    row = jnp.repeat(jnp.arange(rows), GRID_W).astype(F32)
    col = jnp.tile(jnp.arange(GRID_W), rows).astype(F32)
    axis_dim = HEAD // 2
    freqs = ROPE_THETA ** (-jnp.arange(0, axis_dim, 2, dtype=F32) / axis_dim)
    ang = jnp.concatenate([row[:, None] * freqs, col[:, None] * freqs], axis=-1)
    cos, sin = jnp.cos(ang), jnp.sin(ang)
    cosf = jnp.repeat(cos, 2, axis=-1)
    sinf = jnp.stack([-sin, sin], axis=-1).reshape(lr, HEAD)
    return cosf, sinf


def kernel(x_prompt, x_sample, meta_tokens, rel_bias_table, ffn1_norm, ffn1_w_in, ffn1_w_out,
           mix_norm, w_in, diff_lambda, diff_subln, qk_norm, w_out,
           ffn2_norm, ffn2_w_in, ffn2_w_out, final_norm):
    w1i, w1o = ffn1_w_in[0].astype(BF16), ffn1_w_out[0].astype(BF16)
    w2i, w2o = ffn2_w_in[0].astype(BF16), ffn2_w_out[0].astype(BF16)
    wi, wo = w_in[0].astype(BF16), w_out[0].astype(BF16)
    g1, gm, g2 = ffn1_norm, mix_norm, ffn2_norm
    gf = final_norm[None, :]
    qkg = qk_norm[0]
    lam_p = diff_lambda[0]
    subln = diff_subln

    bands, cfar, bias_meta = _bias_tables(rel_bias_table)
    mask_b = jnp.where(jnp.arange(META_PAD)[:, None] < N_META, 0.0, NEG).astype(F32)
    mask_b = jnp.broadcast_to(mask_b, (META_PAD, B_GROUP * B_MQ))

    xm = jnp.zeros((META_PAD, D_MODEL), F32).at[:N_META].set(meta_tokens)
    hm = _ffn(xm, g1, w1i, w1o, gf, tm=META_PAD, final_norm=False)
    ones = jnp.ones((META_PAD, HEAD), F32)
    _, kma, vmat, _, kmb, vmbt = _proj(hm, gm, wi, qkg, ones, jnp.zeros_like(ones), tm=META_PAD)

    def trunk(x):
        batch, lr, d = x.shape
        h = _ffn(x.reshape(batch * lr, d), g1, w1i, w1o, gf, tm=FFN_TM, final_norm=False)
        cosf, sinf = _rope_tables(lr)
        qat, ka, vat, qbt, kb, vbt = _proj(h, gm, wi, qkg, cosf, sinf, tm=PROJ_TM)
        ya = _attn_a(cfar, lam_p, subln, qat, ka, vat, kma, vmat, bias_meta, bands, batch=batch)
        yb = _attn_b(qbt, kb, vbt, kmb, vmbt, mask_b, batch=batch)
        h = _out_proj(h, ya, yb, wo, tm=PROJ_TM)
        y = _ffn(h, g2, w2i, w2o, gf, tm=FFN_TM, final_norm=True)
        return y.reshape(batch, lr, d)

    return trunk(x_prompt), trunk(x_sample)
```

```python
import functools
import math

import jax
import jax.numpy as jnp
from jax import lax
from jax.experimental import pallas as pl
from jax.experimental.pallas import tpu as pltpu

F32 = jnp.float32
BF16 = jnp.bfloat16

EPS = 1e-6
N_META = 16
GRID_W = 64
D_MODEL = 2048
A_HEADS = 8
A_QK = 64
HEAD = 128
B_HEADS = 8
B_KV = 2
B_GROUP = B_HEADS // B_KV
A_COLS = A_HEADS * HEAD
REL_BUCKETS = 32
REL_MAX_DIST = 128
ROPE_THETA = 10000.0
LAM_INIT = 0.8 - 0.6 * math.exp(-0.3 * 0)
LOG2E = math.log2(math.e)
A_QSCALE = A_QK ** -0.5 * LOG2E
B_QSCALE = HEAD ** -0.5 * LOG2E
NEG = -1e30

META_PAD = 128
REL_CLIP = 128

VMEM_LIMIT = 60 * 1024 * 1024

FFN_TM = 1024
FFN_TF = 512
FFN_TN = 512
PROJ_TM = 512
KEY_BLOCK = PROJ_TM
A_MQ = 512
B_MQ = 256
A_NEAR = 4
N_BANDS = 2 * A_NEAR - 1
MAX_EXPONENT = 126.0
BOUND_ROWS = 16
BOUND_SLACK = 1.01
FIXED_REF_GROUP = 8


def _rms(x, g):
    return x * lax.rsqrt(jnp.mean(x * x, axis=-1, keepdims=True) + EPS) * g


def _params(n_grid_dims):
    return pltpu.CompilerParams(
        dimension_semantics=("arbitrary",) * n_grid_dims,
        vmem_limit_bytes=VMEM_LIMIT,
    )


def _ffn_kernel(x_ref, g_ref, wg_ref, wu_ref, wo_ref, fg_ref, o_ref, xn_ref, *, final_norm):
    j = pl.program_id(1)

    @pl.when(j == 0)
    def _():
        x = x_ref[...]
        xn_ref[...] = _rms(x, g_ref[...]).astype(BF16)
        o_ref[...] = x

    xn = xn_ref[...]
    gate = jnp.dot(xn, wg_ref[...], preferred_element_type=F32)
    up = jnp.dot(xn, wu_ref[...], preferred_element_type=F32)
    act = (0.5 * gate / (1.0 + jnp.exp(-gate)) * up).astype(BF16)
    d = o_ref.shape[1]
    for n0 in range(0, d, FFN_TN):
        o_ref[:, n0:n0 + FFN_TN] += jnp.dot(act, wo_ref[:, n0:n0 + FFN_TN],
                                            preferred_element_type=F32)

    if final_norm:
        @pl.when(j == pl.num_programs(1) - 1)
        def _():
            o_ref[...] = _rms(o_ref[...], fg_ref[...])


def _ffn(x, norm_g, w_in, w_out, final_g, *, tm, final_norm):
    n, d = x.shape
    dff = w_out.shape[0]
    nj = dff // FFN_TF
    return pl.pallas_call(
        functools.partial(_ffn_kernel, final_norm=final_norm),
        grid=(n // tm, nj),
        in_specs=[
            pl.BlockSpec((tm, d), lambda i, j: (i, 0)),
            pl.BlockSpec((1, d), lambda i, j: (0, 0)),
            pl.BlockSpec((d, FFN_TF), lambda i, j: (0, j)),
            pl.BlockSpec((d, FFN_TF), lambda i, j: (0, j + nj)),
            pl.BlockSpec((FFN_TF, d), lambda i, j: (j, 0)),
            pl.BlockSpec((1, d), lambda i, j: (0, 0)),
        ],
        out_specs=pl.BlockSpec((tm, d), lambda i, j: (i, 0)),
        out_shape=jax.ShapeDtypeStruct((n, d), F32),
        scratch_shapes=[pltpu.VMEM((tm, d), BF16)],
        compiler_params=_params(2),
        name="ffn_final" if final_norm else "ffn",
    )(x, norm_g, w_in, w_in, w_out, final_g)


def _proj_kernel(h_ref, g_ref, w_ref, qkg_ref, cos_ref, sin_ref,
                 qat_ref, ka_ref, vat_ref, qbt_ref, kb_ref, vbt_ref):
    tm = h_ref.shape[0]
    u = _rms(h_ref[...], g_ref[...]).astype(BF16)
    cosf = cos_ref[...]
    sinf = sin_ref[...]
    lane = lax.broadcasted_iota(jnp.int32, (tm, HEAD), 1)
    even = (lane & 1) == 0

    def proj(c0, width):
        return jnp.dot(u, w_ref[:, c0:c0 + width], preferred_element_type=F32)

    def head(y, hh):
        return y[:, hh * HEAD:(hh + 1) * HEAD]

    def norm_rope(y, g):
        y = _rms(y, g)
        swapped = jnp.where(even, pltpu.roll(y, HEAD - 1, 1), pltpu.roll(y, 1, 1))
        return y * cosf + swapped * sinf

    chunk = 4 * HEAD
    for c in range(2):
        y = proj(c * chunk, chunk)
        for hh in range(4):
            qat_ref[c * 4 + hh] = (head(y, hh) * A_QSCALE).T.astype(BF16)
    for c in range(2):
        ka_ref[:, c * chunk:(c + 1) * chunk] = proj(A_COLS + c * chunk, chunk).astype(BF16)
    for c in range(2):
        y = proj(2 * A_COLS + c * chunk, chunk)
        for hh in range(4):
            vat_ref[c * 4 + hh, 0] = head(y, hh).T.astype(BF16)
    gq = qkg_ref[0:1, :]
    gk = qkg_ref[1:2, :]
    for c in range(2):
        y = proj(3 * A_COLS + c * chunk, chunk)
        for hh in range(4):
            qbt_ref[c * 4 + hh] = (norm_rope(head(y, hh), gq) * B_QSCALE).T.astype(BF16)
    y = proj(4 * A_COLS, chunk)
    for n in range(B_KV):
        kb_ref[:, n * HEAD:(n + 1) * HEAD] = norm_rope(head(y, n), gk).astype(BF16)
        vbt_ref[n, 0] = head(y, B_KV + n).T.astype(BF16)


def _proj(h, norm_g, w_in, qk_g, cosf, sinf, *, tm):
    n, d = h.shape
    nt = n // tm
    ntab = cosf.shape[0] // tm
    in_w = w_in.shape[1]
    out_shape = (
        jax.ShapeDtypeStruct((A_HEADS, HEAD, n), BF16),
        jax.ShapeDtypeStruct((n, A_COLS), BF16),
        jax.ShapeDtypeStruct((A_HEADS, nt, HEAD, tm), BF16),
        jax.ShapeDtypeStruct((B_HEADS, HEAD, n), BF16),
        jax.ShapeDtypeStruct((n, B_KV * HEAD), BF16),
        jax.ShapeDtypeStruct((B_KV, nt, HEAD, tm), BF16),
    )
    return pl.pallas_call(
        _proj_kernel,
        grid=(nt,),
        in_specs=[
            pl.BlockSpec((tm, d), lambda i: (i, 0)),
            pl.BlockSpec((1, d), lambda i: (0, 0)),
            pl.BlockSpec((d, in_w), lambda i: (0, 0), pipeline_mode=pl.Buffered(1)),
            pl.BlockSpec((2, HEAD), lambda i: (0, 0)),
            pl.BlockSpec((tm, HEAD), lambda i: (i % ntab, 0)),
            pl.BlockSpec((tm, HEAD), lambda i: (i % ntab, 0)),
        ],
        out_specs=(
            pl.BlockSpec((A_HEADS, HEAD, tm), lambda i: (0, 0, i)),
            pl.BlockSpec((tm, A_COLS), lambda i: (i, 0)),
            pl.BlockSpec((A_HEADS, 1, HEAD, tm), lambda i: (0, i, 0, 0)),
            pl.BlockSpec((B_HEADS, HEAD, tm), lambda i: (0, 0, i)),
            pl.BlockSpec((tm, B_KV * HEAD), lambda i: (i, 0)),
            pl.BlockSpec((B_KV, 1, HEAD, tm), lambda i: (0, i, 0, 0)),
        ),
        out_shape=out_shape,
        compiler_params=_params(1),
        name="mix_proj",
    )(h, norm_g, w_in, qk_g, cosf, sinf)


def _online_block(s, c, vt, m, l, acc_ref):
    m_new = jnp.maximum(m, jnp.max(s, axis=0, keepdims=True) + c)
    alpha = jnp.exp2(m - m_new)
    p = jnp.exp2(s - (m_new - c))
    l_new = alpha * l + jnp.sum(p, axis=0, keepdims=True)
    acc_ref[...] = alpha * acc_ref[...] + jnp.dot(vt, p.astype(BF16), preferred_element_type=F32)
    return m_new, l_new


def _first_block(s, vt, acc_ref):
    m = jnp.max(s, axis=0, keepdims=True)
    p = jnp.exp2(s - m)
    acc_ref[...] = jnp.dot(vt, p.astype(BF16), preferred_element_type=F32)
    return m, jnp.sum(p, axis=0, keepdims=True)


def _fixed_ref_blocks(n_blocks, block_of, key_scores, off_of, vt_ref, l, acc_ref, p_buf, lead=()):
    group = FIXED_REF_GROUP
    n_loop = max(n_blocks - 1, 0) // group
    first = n_blocks - n_loop * group
    assert group % 2 == 0 and len(lead) + n_blocks > 0

    def value_stage(j, slot):
        acc_ref[...] += jnp.dot(vt_ref[j], p_buf[slot], preferred_element_type=F32)

    def block(slot, s, off, prev, l):
        if prev is not None:
            value_stage(prev, 1 - slot)
        p = jnp.exp2(s - off)
        p_buf[slot] = p.astype(BF16)
        return l + jnp.sum(p, axis=0, keepdims=True)

    prev = None
    for n, (scores_fn, off, j) in enumerate(lead):
        l = block(n % 2, scores_fn(), off, prev, l)
        prev = j
    for u in range(first):
        j = block_of(u)
        l = block((len(lead) + u) % 2, key_scores(j), off_of(j), prev, l)
        prev = j

    def body(g, carry):
        l, prev = carry
        for u in range(group):
            j = block_of(first + g * group + u)
            l = block((len(lead) + first + u) % 2, key_scores(j), off_of(j), prev, l)
            prev = j
        return l, prev

    if n_loop:
        l, prev = lax.fori_loop(0, n_loop, body, (l, jnp.asarray(prev, jnp.int32)))
    value_stage(prev, (len(lead) + n_blocks - 1) % 2)
    return l


def _key_value_bounds(k_ref, vt_ref, kmax_ref, room_ref):
    nk = vt_ref.shape[0]

    def kbody(kb, kmax):
        start = pl.multiple_of(kb * KEY_BLOCK, KEY_BLOCK)
        blk = jnp.abs(k_ref[pl.ds(start, KEY_BLOCK), :].astype(F32))
        return jnp.maximum(kmax, jnp.max(blk, axis=0, keepdims=True))
    kmax = lax.fori_loop(0, nk, kbody, jnp.zeros((1, HEAD), F32))
    kmax_ref[...] = jnp.broadcast_to(kmax, kmax_ref.shape).astype(BF16)

    def vbody(j, vmax):
        return jnp.maximum(vmax, jnp.max(jnp.abs(vt_ref[j].astype(F32)), axis=0, keepdims=True))
    vmax = lax.fori_loop(0, nk, vbody, jnp.ones((1, KEY_BLOCK), F32))
    vmax = jnp.max(vmax, axis=1, keepdims=True)
    room = MAX_EXPONENT - math.log2(nk * KEY_BLOCK) - jnp.log2(vmax)
    room_ref[...] = jnp.broadcast_to(room, room_ref.shape)


def _fits_fixed_reference(kmax_ref, room_ref, qq, c_max, m):
    ub = jnp.dot(kmax_ref[...], jnp.abs(qq), preferred_element_type=F32)[0:1] * BOUND_SLACK
    return jnp.max(ub + c_max - m - room_ref[0:1, 0:1]) <= 0.0


def _pipelined_blocks(qq, k_ref, vt_ref, c_of, m, l, acc_ref, s_buf, p_buf):
    nk = vt_ref.shape[0]
    kb_size = vt_ref.shape[2]
    assert nk % 2 == 0 and nk >= 2

    def stage_scores(j, slot, m):
        start = pl.multiple_of(j * kb_size, kb_size)
        s = jnp.dot(k_ref[pl.ds(start, kb_size), :], qq, preferred_element_type=F32)
        s_buf[slot] = s
        c = c_of(j)
        m_new = jnp.maximum(m, jnp.max(s, axis=0, keepdims=True) + c)
        return m_new, jnp.exp2(m - m_new), m_new - c

    def stage_exp(slot, off, alpha, l):
        p = jnp.exp2(s_buf[slot] - off)
        p_buf[slot] = p.astype(BF16)
        return alpha * l + jnp.sum(p, axis=0, keepdims=True)

    def stage_values(j, slot, alpha):
        acc_ref[...] = alpha * acc_ref[...] + jnp.dot(vt_ref[j], p_buf[slot],
                                                      preferred_element_type=F32)

    m, a0, o0 = stage_scores(0, 0, m)
    l = stage_exp(0, o0, a0, l)
    m, a1, o1 = stage_scores(1, 1, m)

    def pair(g, carry):
        m, l, a_pp, a_p, o_p = carry
        j = 2 * g
        stage_values(j - 2, 0, a_pp)
        l = stage_exp(1, o_p, a_p, l)
        m, a_j, o_j = stage_scores(j, 0, m)
        stage_values(j - 1, 1, a_p)
        l = stage_exp(0, o_j, a_j, l)
        m, a_j1, o_j1 = stage_scores(j + 1, 1, m)
        return m, l, a_j, a_j1, o_j1

    m, l, a_pp, a_p, o_p = lax.fori_loop(1, nk // 2, pair, (m, l, a0, a1, o1))
    stage_values(nk - 2, 0, a_pp)
    l = stage_exp(1, o_p, a_p, l)
    stage_values(nk - 1, 1, a_p)
    return m, l


def _attn_a_kernel(cfar_ref, lam_ref, g_ref, qt_ref, k_ref, vt_ref, km_ref, vmt_ref, bm_ref,
                   band_ref, o_ref, acc_ref, s_buf, p_buf, kmax_ref, room_ref):
    h = pl.program_id(1)
    i = pl.program_id(2)
    mq = qt_ref.shape[1]
    nk = vt_ref.shape[0]
    kb_size = vt_ref.shape[2]

    @pl.when(i == 0)
    def _():
        _key_value_bounds(k_ref, vt_ref, kmax_ref, room_ref)

    qt = qt_ref[...]
    row = lax.broadcasted_iota(jnp.int32, qt.shape, 0)
    zero = jnp.zeros_like(qt)
    qq = jnp.concatenate([jnp.where(row < A_QK, qt, zero), jnp.where(row >= A_QK, qt, zero)], axis=1)

    def scores(k):
        return jnp.dot(k, qq, preferred_element_type=F32)

    def twice(b):
        return jnp.concatenate([b, b], axis=1)

    m, l = _first_block(scores(km_ref[...]) + twice(bm_ref[...]), vmt_ref[...], acc_ref)

    assert mq == kb_size and nk >= A_NEAR
    near = jnp.clip(i - 1, 0, nk - A_NEAR)

    def key_scores(kb):
        start = pl.multiple_of(kb * kb_size, kb_size)
        return scores(k_ref[pl.ds(start, kb_size), :])

    def band_block(n):
        kb = near + n
        return kb, lambda: key_scores(kb) + twice(band_ref[kb - i + A_NEAR - 1])

    c_lo = cfar_ref[3 * h]
    c_hi = cfar_ref[3 * h + 1]
    c_max = cfar_ref[3 * h + 2]

    def fixed_reference(m, l):
        lead = []
        for n in range(A_NEAR):
            kb, scores_fn = band_block(n)
            lead.append((scores_fn, m, kb))
        return _fixed_ref_blocks(nk - A_NEAR, lambda jj: jj + jnp.where(jj >= near, A_NEAR, 0),
                                 key_scores,
                                 lambda j: m - jnp.where(j < near, c_lo, c_hi),
                                 vt_ref, l, acc_ref, p_buf, lead)

    def running_reference(m, l):
        for n in range(A_NEAR):
            kb, scores_fn = band_block(n)
            m, l = _online_block(scores_fn(), 0.0, vt_ref[kb], m, l, acc_ref)

        def c_of(j):
            return jnp.where(j < near, c_lo, jnp.where(j >= near + A_NEAR, c_hi, NEG))
        return _pipelined_blocks(qq, k_ref, vt_ref, c_of, m, l, acc_ref, s_buf, p_buf)[1]

    fits = _fits_fixed_reference(kmax_ref, room_ref, qq, c_max, m)
    l = lax.cond(fits, fixed_reference, running_reference, m, l)

    lp = lam_ref[...]
    lam = (jnp.exp(jnp.sum(lp[0:1] * lp[1:2], axis=1, keepdims=True))
           - jnp.exp(jnp.sum(lp[2:3] * lp[3:4], axis=1, keepdims=True)) + LAM_INIT)
    r = 1.0 / l
    acc = acc_ref[...]
    o = acc[:, :mq] * r[:, :mq] - lam * (acc[:, mq:] * r[:, mq:])
    o_ref[...] = (_rms(o.T, g_ref[...]) * (1.0 - LAM_INIT)).astype(BF16)


def _attn_scratch(width):
    return [pltpu.VMEM((HEAD, width), F32),
            pltpu.VMEM((2, KEY_BLOCK, width), F32),
            pltpu.VMEM((2, KEY_BLOCK, width), BF16),
            pltpu.VMEM((BOUND_ROWS, HEAD), BF16),
            pltpu.VMEM((8, HEAD), F32)]


def _attn_a(cfar, lam_p, subln_g, qat, ka, vat, kma, vmat, bias_meta, bands, *, batch):
    n = ka.shape[0]
    lr = n // batch
    nk = lr // KEY_BLOCK
    nq = lr // A_MQ
    return pl.pallas_call(
        _attn_a_kernel,
        grid=(batch, A_HEADS, nq),
        in_specs=[
            pl.BlockSpec(memory_space=pltpu.SMEM),
            pl.BlockSpec((4, A_QK), lambda b, h, i: (0, 0)),
            pl.BlockSpec((1, HEAD), lambda b, h, i: (0, 0)),
            pl.BlockSpec((None, HEAD, A_MQ), lambda b, h, i: (h, 0, b * nq + i)),
            pl.BlockSpec((lr, HEAD), lambda b, h, i: (b, h)),
            pl.BlockSpec((None, nk, HEAD, KEY_BLOCK), lambda b, h, i: (h, b, 0, 0)),
            pl.BlockSpec((META_PAD, HEAD), lambda b, h, i: (0, h)),
            pl.BlockSpec((None, None, HEAD, META_PAD), lambda b, h, i: (h, 0, 0, 0)),
            pl.BlockSpec((None, None, META_PAD, A_MQ), lambda b, h, i: (h, jnp.minimum(i, 1), 0, 0)),
            pl.BlockSpec((None, N_BANDS, KEY_BLOCK, A_MQ), lambda b, h, i: (h, 0, 0, 0)),
        ],
        out_specs=pl.BlockSpec((A_MQ, HEAD), lambda b, h, i: (b * nq + i, h)),
        out_shape=jax.ShapeDtypeStruct((n, A_COLS), BF16),
        scratch_shapes=_attn_scratch(2 * A_MQ),
        compiler_params=_params(3),
        name="attn_a",
    )(cfar, lam_p, subln_g, qat, ka, vat, kma, vmat, bias_meta, bands)


def _attn_b_kernel(qt_ref, k_ref, vt_ref, km_ref, vmt_ref, mask_ref, o_ref, acc_ref, s_buf, p_buf,
                   kmax_ref, room_ref):
    mq = qt_ref.shape[2]
    nk = vt_ref.shape[0]
    kb_size = vt_ref.shape[2]

    @pl.when(pl.program_id(2) == 0)
    def _():
        _key_value_bounds(k_ref, vt_ref, kmax_ref, room_ref)

    qq = jnp.concatenate([qt_ref[g] for g in range(B_GROUP)], axis=1)
    s = jnp.dot(km_ref[...], qq, preferred_element_type=F32) + mask_ref[...]
    m, l = _first_block(s, vmt_ref[...], acc_ref)

    def key_scores(j):
        start = pl.multiple_of(j * kb_size, kb_size)
        return jnp.dot(k_ref[pl.ds(start, kb_size), :], qq, preferred_element_type=F32)

    def fixed_reference(m, l):
        return _fixed_ref_blocks(nk, lambda jj: jj, key_scores, lambda j: m, vt_ref, l, acc_ref, p_buf)

    def running_reference(m, l):
        return _pipelined_blocks(qq, k_ref, vt_ref, lambda j: 0.0, m, l, acc_ref, s_buf, p_buf)[1]

    fits = _fits_fixed_reference(kmax_ref, room_ref, qq, 0.0, m)
    l = lax.cond(fits, fixed_reference, running_reference, m, l)
    o = acc_ref[...] * (1.0 / l)
    for g in range(B_GROUP):
        o_ref[:, g * HEAD:(g + 1) * HEAD] = o[:, g * mq:(g + 1) * mq].T.astype(BF16)


def _attn_b(qbt, kb, vbt, kmb, vmbt, mask, *, batch):
    n = kb.shape[0]
    lr = n // batch
    nk = lr // KEY_BLOCK
    nq = lr // B_MQ
    return pl.pallas_call(
        _attn_b_kernel,
        grid=(batch, B_KV, nq),
        in_specs=[
            pl.BlockSpec((B_GROUP, HEAD, B_MQ), lambda b, n_, i: (n_, 0, b * nq + i)),
            pl.BlockSpec((lr, HEAD), lambda b, n_, i: (b, n_)),
            pl.BlockSpec((None, nk, HEAD, KEY_BLOCK), lambda b, n_, i: (n_, b, 0, 0)),
            pl.BlockSpec((META_PAD, HEAD), lambda b, n_, i: (0, n_)),
            pl.BlockSpec((None, None, HEAD, META_PAD), lambda b, n_, i: (n_, 0, 0, 0)),
            pl.BlockSpec((META_PAD, B_GROUP * B_MQ), lambda b, n_, i: (0, 0)),
        ],
        out_specs=pl.BlockSpec((B_MQ, B_GROUP * HEAD), lambda b, n_, i: (b * nq + i, n_)),
        out_shape=jax.ShapeDtypeStruct((n, B_HEADS * HEAD), BF16),
        scratch_shapes=_attn_scratch(B_GROUP * B_MQ),
        compiler_params=_params(3),
        name="attn_b",
    )(qbt, kb, vbt, kmb, vmbt, mask)


def _out_kernel(h_ref, ya_ref, yb_ref, w_ref, o_ref):
    o_ref[...] = (h_ref[...]
                  + jnp.dot(ya_ref[...], w_ref[:A_COLS, :], preferred_element_type=F32)
                  + jnp.dot(yb_ref[...], w_ref[A_COLS:, :], preferred_element_type=F32))


def _out_proj(h, ya, yb, w_out, *, tm):
    n, d = h.shape
    return pl.pallas_call(
        _out_kernel,
        grid=(n // tm,),
        in_specs=[
            pl.BlockSpec((tm, d), lambda i: (i, 0)),
            pl.BlockSpec((tm, A_COLS), lambda i: (i, 0)),
            pl.BlockSpec((tm, B_HEADS * HEAD), lambda i: (i, 0)),
            pl.BlockSpec(w_out.shape, lambda i: (0, 0), pipeline_mode=pl.Buffered(1)),
        ],
        out_specs=pl.BlockSpec((tm, d), lambda i: (i, 0)),
        out_shape=jax.ShapeDtypeStruct((n, d), F32),
        compiler_params=_params(1),
        name="out_proj",
    )(h, ya, yb, w_out)


def _t5_bucket(rel):
    half = REL_BUCKETS // 2
    max_exact = half // 2
    n = jnp.abs(rel)
    sign_off = jnp.where(rel > 0, half, 0)
    nf = jnp.maximum(n, 1).astype(F32)
    large = max_exact + (jnp.log(nf / max_exact) / math.log(REL_MAX_DIST / max_exact)
                         * (half - max_exact)).astype(jnp.int32)
    large = jnp.minimum(large, half - 1)
    return sign_off + jnp.where(n < max_exact, n, large)


def _toeplitz_kernel(vec_ref, o_ref):
    rows, cols = o_ref.shape
    x = jnp.broadcast_to(vec_ref[...], (rows, vec_ref.shape[1]))
    o_ref[...] = pltpu.roll(x, 0, 1, stride=1, stride_axis=0)[:, :cols]


def _toeplitz(f, rows, cols):
    period = rows + cols
    x = jnp.arange(period)
    vec = f(jnp.where(x < cols, -x, period - x))
    n = vec.shape[0]
    return pl.pallas_call(
        _toeplitz_kernel,
        grid=(n,),
        in_specs=[pl.BlockSpec((None, 1, period), lambda i: (i, 0, 0))],
        out_specs=pl.BlockSpec((None, rows, cols), lambda i: (i, 0, 0)),
        out_shape=jax.ShapeDtypeStruct((n, rows, cols), F32),
        compiler_params=_params(1),
        name="bias_bands",
    )(vec[:, None, :])


def _bias_tables(rel_table):
    rel1d = jnp.arange(-REL_CLIP, REL_CLIP + 1)
    t1d = rel_table.astype(F32)[_t5_bucket(rel1d)].T * LOG2E

    def lookup(rel):
        return t1d[:, jnp.clip(rel, -REL_CLIP, REL_CLIP) + REL_CLIP]

    delta = ((jnp.arange(N_BANDS) - (A_NEAR - 1)) * KEY_BLOCK)[:, None]
    bands = _toeplitz(lambda y: lookup(delta + y[None, :]).reshape(A_HEADS * N_BANDS, -1),
                      KEY_BLOCK, A_MQ).reshape(A_HEADS, N_BANDS, KEY_BLOCK, A_MQ)

    c_lo = t1d[:, 0]
    c_hi = t1d[:, -1]
    cfar = jnp.stack([c_lo, c_hi, jnp.max(t1d, axis=1)], axis=1).reshape(-1)

    j = jnp.arange(META_PAD)[:, None]
    first = _toeplitz(lambda y: lookup(y - N_META), META_PAD, A_MQ)
    rest = jnp.broadcast_to(c_lo[:, None, None], first.shape)
    bias_meta = jnp.where((j < N_META)[None], jnp.stack([first, rest], axis=1), NEG)
    return bands, cfar, bias_meta


def _rope_tables(lr):
    rows = lr // GRID_W
    row = jnp.repeat(jnp.arange(rows), GRID_W).astype(F32)
    col = jnp.tile(jnp.arange(GRID_W), rows).astype(F32)
    axis_dim = HEAD // 2
    freqs = ROPE_THETA ** (-jnp.arange(0, axis_dim, 2, dtype=F32) / axis_dim)
    ang = jnp.concatenate([row[:, None] * freqs, col[:, None] * freqs], axis=-1)
    cos, sin = jnp.cos(ang), jnp.sin(ang)
    cosf = jnp.repeat(cos, 2, axis=-1)
    sinf = jnp.stack([-sin, sin], axis=-1).reshape(lr, HEAD)
    return cosf, sinf


def kernel(x_prompt, x_sample, meta_tokens, rel_bias_table, ffn1_norm, ffn1_w_in, ffn1_w_out,
           mix_norm, w_in, diff_lambda, diff_subln, qk_norm, w_out,
           ffn2_norm, ffn2_w_in, ffn2_w_out, final_norm):
    w1i, w1o = ffn1_w_in[0].astype(BF16), ffn1_w_out[0].astype(BF16)
    w2i, w2o = ffn2_w_in[0].astype(BF16), ffn2_w_out[0].astype(BF16)
    wi, wo = w_in[0].astype(BF16), w_out[0].astype(BF16)
    g1, gm, g2 = ffn1_norm, mix_norm, ffn2_norm
    gf = final_norm[None, :]
    qkg = qk_norm[0]
    lam_p = diff_lambda[0]
    subln = diff_subln

    bands, cfar, bias_meta = _bias_tables(rel_bias_table)
    mask_b = jnp.where(jnp.arange(META_PAD)[:, None] < N_META, 0.0, NEG).astype(F32)
    mask_b = jnp.broadcast_to(mask_b, (META_PAD, B_GROUP * B_MQ))

    xm = jnp.zeros((META_PAD, D_MODEL), F32).at[:N_META].set(meta_tokens)
    hm = _ffn(xm, g1, w1i, w1o, gf, tm=META_PAD, final_norm=False)
    ones = jnp.ones((META_PAD, HEAD), F32)
    _, kma, vmat, _, kmb, vmbt = _proj(hm, gm, wi, qkg, ones, jnp.zeros_like(ones), tm=META_PAD)

    def trunk(x):
        batch, lr, d = x.shape
        h = _ffn(x.reshape(batch * lr, d), g1, w1i, w1o, gf, tm=FFN_TM, final_norm=False)
        cosf, sinf = _rope_tables(lr)
        qat, ka, vat, qbt, kb, vbt = _proj(h, gm, wi, qkg, cosf, sinf, tm=PROJ_TM)
        ya = _attn_a(cfar, lam_p, subln, qat, ka, vat, kma, vmat, bias_meta, bands, batch=batch)
        yb = _attn_b(qbt, kb, vbt, kmb, vmbt, mask_b, batch=batch)
        h = _out_proj(h, ya, yb, wo, tm=PROJ_TM)
        y = _ffn(h, g2, w2i, w2o, gf, tm=FFN_TM, final_norm=True)
        return y.reshape(batch, lr, d)

    return trunk(x_prompt), trunk(x_sample)
```

```python
import functools
import math

import jax
import jax.numpy as jnp
from jax import lax
from jax.experimental import pallas as pl
from jax.experimental.pallas import tpu as pltpu

F32 = jnp.float32
BF16 = jnp.bfloat16

EPS = 1e-6
N_META = 16
GRID_W = 64
D_MODEL = 2048
A_HEADS = 8
A_QK = 64
HEAD = 128
B_HEADS = 8
B_KV = 2
B_GROUP = B_HEADS // B_KV
A_COLS = A_HEADS * HEAD
REL_BUCKETS = 32
REL_MAX_DIST = 128
ROPE_THETA = 10000.0
LAM_INIT = 0.8 - 0.6 * math.exp(-0.3 * 0)
LOG2E = math.log2(math.e)
A_QSCALE = A_QK ** -0.5 * LOG2E
B_QSCALE = HEAD ** -0.5 * LOG2E
NEG = -1e30

META_PAD = 128
REL_CLIP = 128

VMEM_LIMIT = 60 * 1024 * 1024

FFN_TM = 1024
FFN_TF = 512
FFN_TN = 512
PROJ_TM = 512
KEY_BLOCK = PROJ_TM
A_MQ = 512
B_MQ = 256
A_NEAR = 4
N_BANDS = 2 * A_NEAR - 1
MAX_EXPONENT = 126.0
BOUND_ROWS = 16
BOUND_SLACK = 1.01
MIN_SUM = 2.0 ** -60
FIXED_REF_GROUP = 8


def _rms(x, g):
    return x * lax.rsqrt(jnp.mean(x * x, axis=-1, keepdims=True) + EPS) * g


def _params(n_grid_dims):
    return pltpu.CompilerParams(
        dimension_semantics=("arbitrary",) * n_grid_dims,
        vmem_limit_bytes=VMEM_LIMIT,
    )


def _ffn_kernel(x_ref, g_ref, wg_ref, wu_ref, wo_ref, fg_ref, o_ref, xn_ref, *, final_norm):
    j = pl.program_id(1)

    @pl.when(j == 0)
    def _():
        x = x_ref[...]
        xn_ref[...] = _rms(x, g_ref[...]).astype(BF16)
        o_ref[...] = x

    xn = xn_ref[...]
    gate = jnp.dot(xn, wg_ref[...], preferred_element_type=F32)
    up = jnp.dot(xn, wu_ref[...], preferred_element_type=F32)
    act = (0.5 * gate / (1.0 + jnp.exp(-gate)) * up).astype(BF16)
    d = o_ref.shape[1]
    for n0 in range(0, d, FFN_TN):
        o_ref[:, n0:n0 + FFN_TN] += jnp.dot(act, wo_ref[:, n0:n0 + FFN_TN],
                                            preferred_element_type=F32)

    if final_norm:
        @pl.when(j == pl.num_programs(1) - 1)
        def _():
            o_ref[...] = _rms(o_ref[...], fg_ref[...])


def _ffn(x, norm_g, w_in, w_out, final_g, *, tm, final_norm):
    n, d = x.shape
    dff = w_out.shape[0]
    nj = dff // FFN_TF
    return pl.pallas_call(
        functools.partial(_ffn_kernel, final_norm=final_norm),
        grid=(n // tm, nj),
        in_specs=[
            pl.BlockSpec((tm, d), lambda i, j: (i, 0)),
            pl.BlockSpec((1, d), lambda i, j: (0, 0)),
            pl.BlockSpec((d, FFN_TF), lambda i, j: (0, j)),
            pl.BlockSpec((d, FFN_TF), lambda i, j: (0, j + nj)),
            pl.BlockSpec((FFN_TF, d), lambda i, j: (j, 0)),
            pl.BlockSpec((1, d), lambda i, j: (0, 0)),
        ],
        out_specs=pl.BlockSpec((tm, d), lambda i, j: (i, 0)),
        out_shape=jax.ShapeDtypeStruct((n, d), F32),
        scratch_shapes=[pltpu.VMEM((tm, d), BF16)],
        compiler_params=_params(2),
        name="ffn_final" if final_norm else "ffn",
    )(x, norm_g, w_in, w_in, w_out, final_g)


def _proj_kernel(h_ref, g_ref, w_ref, qkg_ref, cos_ref, sin_ref,
                 qat_ref, ka_ref, vat_ref, qbt_ref, kb_ref, vbt_ref):
    tm = h_ref.shape[0]
    u = _rms(h_ref[...], g_ref[...]).astype(BF16)
    cosf = cos_ref[...]
    sinf = sin_ref[...]
    lane = lax.broadcasted_iota(jnp.int32, (tm, HEAD), 1)
    even = (lane & 1) == 0

    def proj(c0, width):
        return jnp.dot(u, w_ref[:, c0:c0 + width], preferred_element_type=F32)

    def head(y, hh):
        return y[:, hh * HEAD:(hh + 1) * HEAD]

    def norm_rope(y, g):
        y = _rms(y, g)
        swapped = jnp.where(even, pltpu.roll(y, HEAD - 1, 1), pltpu.roll(y, 1, 1))
        return y * cosf + swapped * sinf

    chunk = 4 * HEAD
    for c in range(2):
        y = proj(c * chunk, chunk)
        for hh in range(4):
            qat_ref[c * 4 + hh] = (head(y, hh) * A_QSCALE).T.astype(BF16)
    for c in range(2):
        ka_ref[:, c * chunk:(c + 1) * chunk] = proj(A_COLS + c * chunk, chunk).astype(BF16)
    for c in range(2):
        y = proj(2 * A_COLS + c * chunk, chunk)
        for hh in range(4):
            vat_ref[c * 4 + hh, 0] = head(y, hh).T.astype(BF16)
    gq = qkg_ref[0:1, :]
    gk = qkg_ref[1:2, :]
    for c in range(2):
        y = proj(3 * A_COLS + c * chunk, chunk)
        for hh in range(4):
            qbt_ref[c * 4 + hh] = (norm_rope(head(y, hh), gq) * B_QSCALE).T.astype(BF16)
    y = proj(4 * A_COLS, chunk)
    for n in range(B_KV):
        kb_ref[:, n * HEAD:(n + 1) * HEAD] = norm_rope(head(y, n), gk).astype(BF16)
        vbt_ref[n, 0] = head(y, B_KV + n).T.astype(BF16)


def _proj(h, norm_g, w_in, qk_g, cosf, sinf, *, tm):
    n, d = h.shape
    nt = n // tm
    ntab = cosf.shape[0] // tm
    in_w = w_in.shape[1]
    out_shape = (
        jax.ShapeDtypeStruct((A_HEADS, HEAD, n), BF16),
        jax.ShapeDtypeStruct((n, A_COLS), BF16),
        jax.ShapeDtypeStruct((A_HEADS, nt, HEAD, tm), BF16),
        jax.ShapeDtypeStruct((B_HEADS, HEAD, n), BF16),
        jax.ShapeDtypeStruct((n, B_KV * HEAD), BF16),
        jax.ShapeDtypeStruct((B_KV, nt, HEAD, tm), BF16),
    )
    return pl.pallas_call(
        _proj_kernel,
        grid=(nt,),
        in_specs=[
            pl.BlockSpec((tm, d), lambda i: (i, 0)),
            pl.BlockSpec((1, d), lambda i: (0, 0)),
            pl.BlockSpec((d, in_w), lambda i: (0, 0), pipeline_mode=pl.Buffered(1)),
            pl.BlockSpec((2, HEAD), lambda i: (0, 0)),
            pl.BlockSpec((tm, HEAD), lambda i: (i % ntab, 0)),
            pl.BlockSpec((tm, HEAD), lambda i: (i % ntab, 0)),
        ],
        out_specs=(
            pl.BlockSpec((A_HEADS, HEAD, tm), lambda i: (0, 0, i)),
            pl.BlockSpec((tm, A_COLS), lambda i: (i, 0)),
            pl.BlockSpec((A_HEADS, 1, HEAD, tm), lambda i: (0, i, 0, 0)),
            pl.BlockSpec((B_HEADS, HEAD, tm), lambda i: (0, 0, i)),
            pl.BlockSpec((tm, B_KV * HEAD), lambda i: (i, 0)),
            pl.BlockSpec((B_KV, 1, HEAD, tm), lambda i: (0, i, 0, 0)),
        ),
        out_shape=out_shape,
        compiler_params=_params(1),
        name="mix_proj",
    )(h, norm_g, w_in, qk_g, cosf, sinf)


def _online_block(s, c, vt, m, l, acc_ref):
    m_new = jnp.maximum(m, jnp.max(s, axis=0, keepdims=True) + c)
    alpha = jnp.exp2(m - m_new)
    p = jnp.exp2(s - (m_new - c))
    l_new = alpha * l + jnp.sum(p, axis=0, keepdims=True)
    acc_ref[...] = alpha * acc_ref[...] + jnp.dot(vt, p.astype(BF16), preferred_element_type=F32)
    return m_new, l_new


def _first_block(s, vt, acc_ref):
    m = jnp.max(s, axis=0, keepdims=True)
    p = jnp.exp2(s - m)
    acc_ref[...] = jnp.dot(vt, p.astype(BF16), preferred_element_type=F32)
    return m, jnp.sum(p, axis=0, keepdims=True)


def _fixed_ref_blocks(lead, n_blocks, block_of, key_scores, off_of, vt_ref, acc_ref, p_buf):
    group = FIXED_REF_GROUP
    n_loop = max(n_blocks - 1, 0) // group
    first = n_blocks - n_loop * group
    assert group % 2 == 0 and len(lead) + first >= 2
    acc_started = []

    def values_of(prev):
        if callable(prev):
            return prev
        return lambda slot: jnp.dot(vt_ref[prev], p_buf[slot], preferred_element_type=F32)

    def add_values(prev, slot):
        prod = values_of(prev)(slot)
        if acc_started:
            acc_ref[...] += prod
        else:
            acc_ref[...] = prod
            acc_started.append(True)

    def block(slot, s, off, prev, l):
        if prev is not None:
            add_values(prev, 1 - slot)
        p = jnp.exp2(s - off)
        p_buf[slot, :s.shape[0]] = p.astype(BF16)
        return l + jnp.sum(p, axis=0, keepdims=True)

    l = jnp.zeros((1, acc_ref.shape[1]), F32)
    prev = None
    for n, (scores_fn, off, values_fn) in enumerate(lead):
        l = block(n % 2, scores_fn(), off, prev, l)
        prev = values_fn
    for u in range(first):
        j = block_of(u)
        l = block((len(lead) + u) % 2, key_scores(j), off_of(j), prev, l)
        prev = j

    def body(g, carry):
        l, prev = carry
        for u in range(group):
            j = block_of(first + g * group + u)
            l = block((len(lead) + first + u) % 2, key_scores(j), off_of(j), prev, l)
            prev = j
        return l, prev

    if n_loop:
        l, prev = lax.fori_loop(0, n_loop, body, (l, jnp.asarray(prev, jnp.int32)))
    add_values(prev, (len(lead) + n_blocks - 1) % 2)
    return l


def _key_value_bounds(k_ref, vt_ref, kmax_ref, room_ref):
    nk = vt_ref.shape[0]

    def kbody(kb, kmax):
        start = pl.multiple_of(kb * KEY_BLOCK, KEY_BLOCK)
        blk = jnp.abs(k_ref[pl.ds(start, KEY_BLOCK), :].astype(F32))
        return jnp.maximum(kmax, jnp.max(blk, axis=0, keepdims=True))
    kmax = lax.fori_loop(0, nk, kbody, jnp.zeros((1, HEAD), F32))
    kmax_ref[...] = jnp.broadcast_to(kmax, kmax_ref.shape).astype(BF16)

    def vbody(j, vmax):
        return jnp.maximum(vmax, jnp.max(jnp.abs(vt_ref[j].astype(F32)), axis=0, keepdims=True))
    vmax = lax.fori_loop(0, nk, vbody, jnp.ones((1, KEY_BLOCK), F32))
    vmax = jnp.max(vmax, axis=1, keepdims=True)
    room = MAX_EXPONENT - math.log2(nk * KEY_BLOCK) - jnp.log2(vmax)
    room_ref[...] = jnp.broadcast_to(room, room_ref.shape)


def _fixed_reference(kmax_ref, room_ref, qq, c_max):
    ub = jnp.dot(kmax_ref[...], jnp.abs(qq), preferred_element_type=F32)[0:1] * BOUND_SLACK
    return ub + c_max - room_ref[0:1, 0:1]


def _pipelined_blocks(qq, k_ref, vt_ref, c_of, m, l, acc_ref, s_buf, p_buf):
    nk = vt_ref.shape[0]
    kb_size = vt_ref.shape[2]
    assert nk % 2 == 0 and nk >= 2

    def stage_scores(j, slot, m):
        start = pl.multiple_of(j * kb_size, kb_size)
        s = jnp.dot(k_ref[pl.ds(start, kb_size), :], qq, preferred_element_type=F32)
        s_buf[slot] = s
        c = c_of(j)
        m_new = jnp.maximum(m, jnp.max(s, axis=0, keepdims=True) + c)
        return m_new, jnp.exp2(m - m_new), m_new - c

    def stage_exp(slot, off, alpha, l):
        p = jnp.exp2(s_buf[slot] - off)
        p_buf[slot] = p.astype(BF16)
        return alpha * l + jnp.sum(p, axis=0, keepdims=True)

    def stage_values(j, slot, alpha):
        acc_ref[...] = alpha * acc_ref[...] + jnp.dot(vt_ref[j], p_buf[slot],
                                                      preferred_element_type=F32)

    m, a0, o0 = stage_scores(0, 0, m)
    l = stage_exp(0, o0, a0, l)
    m, a1, o1 = stage_scores(1, 1, m)

    def pair(g, carry):
        m, l, a_pp, a_p, o_p = carry
        j = 2 * g
        stage_values(j - 2, 0, a_pp)
        l = stage_exp(1, o_p, a_p, l)
        m, a_j, o_j = stage_scores(j, 0, m)
        stage_values(j - 1, 1, a_p)
        l = stage_exp(0, o_j, a_j, l)
        m, a_j1, o_j1 = stage_scores(j + 1, 1, m)
        return m, l, a_j, a_j1, o_j1

    m, l, a_pp, a_p, o_p = lax.fori_loop(1, nk // 2, pair, (m, l, a0, a1, o1))
    stage_values(nk - 2, 0, a_pp)
    l = stage_exp(1, o_p, a_p, l)
    stage_values(nk - 1, 1, a_p)
    return m, l


def _attn_a_kernel(cfar_ref, lam_ref, g_ref, qt_ref, k_ref, vt_ref, km_ref, vmt_ref, bm_ref,
                   band_ref, o_ref, acc_ref, s_buf, p_buf, kmax_ref, room_ref):
    h = pl.program_id(1)
    i = pl.program_id(2)
    mq = qt_ref.shape[1]
    nk = vt_ref.shape[0]
    kb_size = vt_ref.shape[2]

    @pl.when(i == 0)
    def _():
        _key_value_bounds(k_ref, vt_ref, kmax_ref, room_ref)

    qt = qt_ref[...]
    row = lax.broadcasted_iota(jnp.int32, qt.shape, 0)
    zero = jnp.zeros_like(qt)
    qq = jnp.concatenate([jnp.where(row < A_QK, qt, zero), jnp.where(row >= A_QK, qt, zero)], axis=1)

    def scores(k):
        return jnp.dot(k, qq, preferred_element_type=F32)

    def twice(b):
        return jnp.concatenate([b, b], axis=1)

    def meta_scores():
        return scores(km_ref[...]) + twice(bm_ref[...])

    assert mq == kb_size and nk >= A_NEAR
    near = jnp.clip(i - 1, 0, nk - A_NEAR)

    def key_scores(kb):
        start = pl.multiple_of(kb * kb_size, kb_size)
        return scores(k_ref[pl.ds(start, kb_size), :])

    def band_block(n):
        kb = near + n
        return kb, lambda: key_scores(kb) + twice(band_ref[kb - i + A_NEAR - 1])

    c_lo = cfar_ref[3 * h]
    c_hi = cfar_ref[3 * h + 1]
    c_max = cfar_ref[3 * h + 2]

    m = _fixed_reference(kmax_ref, room_ref, qq, c_max)
    lead = [(meta_scores, m,
             lambda slot: jnp.dot(vmt_ref[...], p_buf[slot, :META_PAD], preferred_element_type=F32))]
    for n in range(A_NEAR):
        kb, scores_fn = band_block(n)
        lead.append((scores_fn, m, kb))
    l = _fixed_ref_blocks(lead, nk - A_NEAR, lambda jj: jj + jnp.where(jj >= near, A_NEAR, 0),
                          key_scores, lambda j: m - jnp.where(j < near, c_lo, c_hi),
                          vt_ref, acc_ref, p_buf)

    def running_reference(_):
        m, l = _first_block(meta_scores(), vmt_ref[...], acc_ref)
        for n in range(A_NEAR):
            kb, scores_fn = band_block(n)
            m, l = _online_block(scores_fn(), 0.0, vt_ref[kb], m, l, acc_ref)

        def c_of(j):
            return jnp.where(j < near, c_lo, jnp.where(j >= near + A_NEAR, c_hi, NEG))
        return _pipelined_blocks(qq, k_ref, vt_ref, c_of, m, l, acc_ref, s_buf, p_buf)[1]

    l = lax.cond(jnp.min(l) >= MIN_SUM, lambda l: l, running_reference, l)

    lp = lam_ref[...]
    lam = (jnp.exp(jnp.sum(lp[0:1] * lp[1:2], axis=1, keepdims=True))
           - jnp.exp(jnp.sum(lp[2:3] * lp[3:4], axis=1, keepdims=True)) + LAM_INIT)
    r = 1.0 / l
    acc = acc_ref[...]
    o = acc[:, :mq] * r[:, :mq] - lam * (acc[:, mq:] * r[:, mq:])
    o_ref[...] = (_rms(o.T, g_ref[...]) * (1.0 - LAM_INIT)).astype(BF16)


def _attn_scratch(width):
    return [pltpu.VMEM((HEAD, width), F32),
            pltpu.VMEM((2, KEY_BLOCK, width), F32),
            pltpu.VMEM((2, KEY_BLOCK, width), BF16),
            pltpu.VMEM((BOUND_ROWS, HEAD), BF16),
            pltpu.VMEM((8, HEAD), F32)]


def _attn_a(cfar, lam_p, subln_g, qat, ka, vat, kma, vmat, bias_meta, bands, *, batch):
    n = ka.shape[0]
    lr = n // batch
    nk = lr // KEY_BLOCK
    nq = lr // A_MQ
    return pl.pallas_call(
        _attn_a_kernel,
        grid=(batch, A_HEADS, nq),
        in_specs=[
            pl.BlockSpec(memory_space=pltpu.SMEM),
            pl.BlockSpec((4, A_QK), lambda b, h, i: (0, 0)),
            pl.BlockSpec((1, HEAD), lambda b, h, i: (0, 0)),
            pl.BlockSpec((None, HEAD, A_MQ), lambda b, h, i: (h, 0, b * nq + i)),
            pl.BlockSpec((lr, HEAD), lambda b, h, i: (b, h)),
            pl.BlockSpec((None, nk, HEAD, KEY_BLOCK), lambda b, h, i: (h, b, 0, 0)),
            pl.BlockSpec((META_PAD, HEAD), lambda b, h, i: (0, h)),
            pl.BlockSpec((None, None, HEAD, META_PAD), lambda b, h, i: (h, 0, 0, 0)),
            pl.BlockSpec((None, None, META_PAD, A_MQ), lambda b, h, i: (h, jnp.minimum(i, 1), 0, 0)),
            pl.BlockSpec((None, N_BANDS, KEY_BLOCK, A_MQ), lambda b, h, i: (h, 0, 0, 0)),
        ],
        out_specs=pl.BlockSpec((A_MQ, HEAD), lambda b, h, i: (b * nq + i, h)),
        out_shape=jax.ShapeDtypeStruct((n, A_COLS), BF16),
        scratch_shapes=_attn_scratch(2 * A_MQ),
        compiler_params=_params(3),
        name="attn_a",
    )(cfar, lam_p, subln_g, qat, ka, vat, kma, vmat, bias_meta, bands)


def _attn_b_kernel(qt_ref, k_ref, vt_ref, km_ref, vmt_ref, mask_ref, o_ref, acc_ref, s_buf, p_buf,
                   kmax_ref, room_ref):
    mq = qt_ref.shape[2]
    nk = vt_ref.shape[0]
    kb_size = vt_ref.shape[2]

    @pl.when(pl.program_id(2) == 0)
    def _():
        _key_value_bounds(k_ref, vt_ref, kmax_ref, room_ref)

    qq = jnp.concatenate([qt_ref[g] for g in range(B_GROUP)], axis=1)
    def meta_scores():
        return jnp.dot(km_ref[...], qq, preferred_element_type=F32) + mask_ref[...]

    def key_scores(j):
        start = pl.multiple_of(j * kb_size, kb_size)
        return jnp.dot(k_ref[pl.ds(start, kb_size), :], qq, preferred_element_type=F32)

    m = _fixed_reference(kmax_ref, room_ref, qq, 0.0)
    lead = [(meta_scores, m,
             lambda slot: jnp.dot(vmt_ref[...], p_buf[slot, :META_PAD], preferred_element_type=F32))]
    l = _fixed_ref_blocks(lead, nk, lambda jj: jj, key_scores, lambda j: m, vt_ref, acc_ref, p_buf)

    def running_reference(_):
        m, l = _first_block(meta_scores(), vmt_ref[...], acc_ref)
        return _pipelined_blocks(qq, k_ref, vt_ref, lambda j: 0.0, m, l, acc_ref, s_buf, p_buf)[1]

    l = lax.cond(jnp.min(l) >= MIN_SUM, lambda l: l, running_reference, l)
    o = acc_ref[...] * (1.0 / l)
    for g in range(B_GROUP):
        o_ref[:, g * HEAD:(g + 1) * HEAD] = o[:, g * mq:(g + 1) * mq].T.astype(BF16)


def _attn_b(qbt, kb, vbt, kmb, vmbt, mask, *, batch):
    n = kb.shape[0]
    lr = n // batch
    nk = lr // KEY_BLOCK
    nq = lr // B_MQ
    return pl.pallas_call(
        _attn_b_kernel,
        grid=(batch, B_KV, nq),
        in_specs=[
            pl.BlockSpec((B_GROUP, HEAD, B_MQ), lambda b, n_, i: (n_, 0, b * nq + i)),
            pl.BlockSpec((lr, HEAD), lambda b, n_, i: (b, n_)),
            pl.BlockSpec((None, nk, HEAD, KEY_BLOCK), lambda b, n_, i: (n_, b, 0, 0)),
            pl.BlockSpec((META_PAD, HEAD), lambda b, n_, i: (0, n_)),
            pl.BlockSpec((None, None, HEAD, META_PAD), lambda b, n_, i: (n_, 0, 0, 0)),
            pl.BlockSpec((META_PAD, B_GROUP * B_MQ), lambda b, n_, i: (0, 0)),
        ],
        out_specs=pl.BlockSpec((B_MQ, B_GROUP * HEAD), lambda b, n_, i: (b * nq + i, n_)),
        out_shape=jax.ShapeDtypeStruct((n, B_HEADS * HEAD), BF16),
        scratch_shapes=_attn_scratch(B_GROUP * B_MQ),
        compiler_params=_params(3),
        name="attn_b",
    )(qbt, kb, vbt, kmb, vmbt, mask)


def _out_kernel(h_ref, ya_ref, yb_ref, w_ref, o_ref):
    o_ref[...] = (h_ref[...]
                  + jnp.dot(ya_ref[...], w_ref[:A_COLS, :], preferred_element_type=F32)
                  + jnp.dot(yb_ref[...], w_ref[A_COLS:, :], preferred_element_type=F32))


def _out_proj(h, ya, yb, w_out, *, tm):
    n, d = h.shape
    return pl.pallas_call(
        _out_kernel,
        grid=(n // tm,),
        in_specs=[
            pl.BlockSpec((tm, d), lambda i: (i, 0)),
            pl.BlockSpec((tm, A_COLS), lambda i: (i, 0)),
            pl.BlockSpec((tm, B_HEADS * HEAD), lambda i: (i, 0)),
            pl.BlockSpec(w_out.shape, lambda i: (0, 0), pipeline_mode=pl.Buffered(1)),
        ],
        out_specs=pl.BlockSpec((tm, d), lambda i: (i, 0)),
        out_shape=jax.ShapeDtypeStruct((n, d), F32),
        compiler_params=_params(1),
        name="out_proj",
    )(h, ya, yb, w_out)


def _t5_bucket(rel):
    half = REL_BUCKETS // 2
    max_exact = half // 2
    n = jnp.abs(rel)
    sign_off = jnp.where(rel > 0, half, 0)
    nf = jnp.maximum(n, 1).astype(F32)
    large = max_exact + (jnp.log(nf / max_exact) / math.log(REL_MAX_DIST / max_exact)
                         * (half - max_exact)).astype(jnp.int32)
    large = jnp.minimum(large, half - 1)
    return sign_off + jnp.where(n < max_exact, n, large)


def _toeplitz_kernel(vec_ref, o_ref):
    rows, cols = o_ref.shape
    x = jnp.broadcast_to(vec_ref[...], (rows, vec_ref.shape[1]))
    o_ref[...] = pltpu.roll(x, 0, 1, stride=1, stride_axis=0)[:, :cols]


def _toeplitz(f, rows, cols):
    period = rows + cols
    x = jnp.arange(period)
    vec = f(jnp.where(x < cols, -x, period - x))
    n = vec.shape[0]
    return pl.pallas_call(
        _toeplitz_kernel,
        grid=(n,),
        in_specs=[pl.BlockSpec((None, 1, period), lambda i: (i, 0, 0))],
        out_specs=pl.BlockSpec((None, rows, cols), lambda i: (i, 0, 0)),
        out_shape=jax.ShapeDtypeStruct((n, rows, cols), F32),
        compiler_params=_params(1),
        name="bias_bands",
    )(vec[:, None, :])


def _bias_tables(rel_table):
    rel1d = jnp.arange(-REL_CLIP, REL_CLIP + 1)
    t1d = rel_table.astype(F32)[_t5_bucket(rel1d)].T * LOG2E

    def lookup(rel):
        return t1d[:, jnp.clip(rel, -REL_CLIP, REL_CLIP) + REL_CLIP]

    delta = ((jnp.arange(N_BANDS) - (A_NEAR - 1)) * KEY_BLOCK)[:, None]
    bands = _toeplitz(lambda y: lookup(delta + y[None, :]).reshape(A_HEADS * N_BANDS, -1),
                      KEY_BLOCK, A_MQ).reshape(A_HEADS, N_BANDS, KEY_BLOCK, A_MQ)

    c_lo = t1d[:, 0]
    c_hi = t1d[:, -1]
    cfar = jnp.stack([c_lo, c_hi, jnp.max(t1d, axis=1)], axis=1).reshape(-1)

    j = jnp.arange(META_PAD)[:, None]
    first = _toeplitz(lambda y: lookup(y - N_META), META_PAD, A_MQ)
    rest = jnp.broadcast_to(c_lo[:, None, None], first.shape)
    bias_meta = jnp.where((j < N_META)[None], jnp.stack([first, rest], axis=1), NEG)
    return bands, cfar, bias_meta


def _rope_tables(lr):
    rows = lr // GRID_W
    row = jnp.repeat(jnp.arange(rows), GRID_W).astype(F32)
    col = jnp.tile(jnp.arange(GRID_W), rows).astype(F32)
    axis_dim = HEAD // 2
    freqs = ROPE_THETA ** (-jnp.arange(0, axis_dim, 2, dtype=F32) / axis_dim)
    ang = jnp.concatenate([row[:, None] * freqs, col[:, None] * freqs], axis=-1)
    cos, sin = jnp.cos(ang), jnp.sin(ang)
    cosf = jnp.repeat(cos, 2, axis=-1)
    sinf = jnp.stack([-sin, sin], axis=-1).reshape(lr, HEAD)
    return cosf, sinf


def kernel(x_prompt, x_sample, meta_tokens, rel_bias_table, ffn1_norm, ffn1_w_in, ffn1_w_out,
           mix_norm, w_in, diff_lambda, diff_subln, qk_norm, w_out,
           ffn2_norm, ffn2_w_in, ffn2_w_out, final_norm):
    w1i, w1o = ffn1_w_in[0].astype(BF16), ffn1_w_out[0].astype(BF16)
    w2i, w2o = ffn2_w_in[0].astype(BF16), ffn2_w_out[0].astype(BF16)
    wi, wo = w_in[0].astype(BF16), w_out[0].astype(BF16)
    g1, gm, g2 = ffn1_norm, mix_norm, ffn2_norm
    gf = final_norm[None, :]
    qkg = qk_norm[0]
    lam_p = diff_lambda[0]
    subln = diff_subln

    bands, cfar, bias_meta = _bias_tables(rel_bias_table)
    mask_b = jnp.where(jnp.arange(META_PAD)[:, None] < N_META, 0.0, NEG).astype(F32)
    mask_b = jnp.broadcast_to(mask_b, (META_PAD, B_GROUP * B_MQ))

    xm = jnp.zeros((META_PAD, D_MODEL), F32).at[:N_META].set(meta_tokens)
    hm = _ffn(xm, g1, w1i, w1o, gf, tm=META_PAD, final_norm=False)
    ones = jnp.ones((META_PAD, HEAD), F32)
    _, kma, vmat, _, kmb, vmbt = _proj(hm, gm, wi, qkg, ones, jnp.zeros_like(ones), tm=META_PAD)

    def trunk(x):
        batch, lr, d = x.shape
        h = _ffn(x.reshape(batch * lr, d), g1, w1i, w1o, gf, tm=FFN_TM, final_norm=False)
        cosf, sinf = _rope_tables(lr)
        qat, ka, vat, qbt, kb, vbt = _proj(h, gm, wi, qkg, cosf, sinf, tm=PROJ_TM)
        ya = _attn_a(cfar, lam_p, subln, qat, ka, vat, kma, vmat, bias_meta, bands, batch=batch)
        yb = _attn_b(qbt, kb, vbt, kmb, vmbt, mask_b, batch=batch)
        h = _out_proj(h, ya, yb, wo, tm=PROJ_TM)
        y = _ffn(h, g2, w2i, w2o, gf, tm=FFN_TM, final_norm=True)
        return y.reshape(batch, lr, d)

    return trunk(x_prompt), trunk(x_sample)
```

```python
import functools
import math

import jax
import jax.numpy as jnp
from jax import lax
from jax.experimental import pallas as pl
from jax.experimental.pallas import tpu as pltpu

F32 = jnp.float32
BF16 = jnp.bfloat16

EPS = 1e-6
N_META = 16
GRID_W = 64
D_MODEL = 2048
A_HEADS = 8
A_QK = 64
HEAD = 128
B_HEADS = 8
B_KV = 2
B_GROUP = B_HEADS // B_KV
A_COLS = A_HEADS * HEAD
REL_BUCKETS = 32
REL_MAX_DIST = 128
ROPE_THETA = 10000.0
LAM_INIT = 0.8 - 0.6 * math.exp(-0.3 * 0)
LOG2E = math.log2(math.e)
A_QSCALE = A_QK ** -0.5 * LOG2E
B_QSCALE = HEAD ** -0.5 * LOG2E
NEG = -1e30

META_PAD = 128
REL_CLIP = 128

VMEM_LIMIT = 60 * 1024 * 1024

FFN_TM = 1024
FFN_TF = 512
FFN_TN = 512
PROJ_TM = 512
KEY_BLOCK = PROJ_TM
A_MQ = 512
B_MQ = 256
A_NEAR = 4
N_BANDS = 2 * A_NEAR - 1
MAX_EXPONENT = 126.0
BOUND_ROWS = 16
BOUND_SLACK = 1.01
MIN_SUM = 2.0 ** -60
FIXED_REF_GROUP = 16


def _rms(x, g):
    return x * lax.rsqrt(jnp.mean(x * x, axis=-1, keepdims=True) + EPS) * g


def _params(n_grid_dims):
    return pltpu.CompilerParams(
        dimension_semantics=("arbitrary",) * n_grid_dims,
        vmem_limit_bytes=VMEM_LIMIT,
    )


def _ffn_kernel(x_ref, g_ref, wg_ref, wu_ref, wo_ref, fg_ref, o_ref, xn_ref, *, final_norm):
    j = pl.program_id(1)

    @pl.when(j == 0)
    def _():
        x = x_ref[...]
        xn_ref[...] = _rms(x, g_ref[...]).astype(BF16)
        o_ref[...] = x

    xn = xn_ref[...]
    gate = jnp.dot(xn, wg_ref[...], preferred_element_type=F32)
    up = jnp.dot(xn, wu_ref[...], preferred_element_type=F32)
    act = (0.5 * gate / (1.0 + jnp.exp(-gate)) * up).astype(BF16)
    d = o_ref.shape[1]
    for n0 in range(0, d, FFN_TN):
        o_ref[:, n0:n0 + FFN_TN] += jnp.dot(act, wo_ref[:, n0:n0 + FFN_TN],
                                            preferred_element_type=F32)

    if final_norm:
        @pl.when(j == pl.num_programs(1) - 1)
        def _():
            o_ref[...] = _rms(o_ref[...], fg_ref[...])


def _ffn(x, norm_g, w_in, w_out, final_g, *, tm, final_norm):
    n, d = x.shape
    dff = w_out.shape[0]
    nj = dff // FFN_TF
    return pl.pallas_call(
        functools.partial(_ffn_kernel, final_norm=final_norm),
        grid=(n // tm, nj),
        in_specs=[
            pl.BlockSpec((tm, d), lambda i, j: (i, 0)),
            pl.BlockSpec((1, d), lambda i, j: (0, 0)),
            pl.BlockSpec((d, FFN_TF), lambda i, j: (0, j)),
            pl.BlockSpec((d, FFN_TF), lambda i, j: (0, j + nj)),
            pl.BlockSpec((FFN_TF, d), lambda i, j: (j, 0)),
            pl.BlockSpec((1, d), lambda i, j: (0, 0)),
        ],
        out_specs=pl.BlockSpec((tm, d), lambda i, j: (i, 0)),
        out_shape=jax.ShapeDtypeStruct((n, d), F32),
        scratch_shapes=[pltpu.VMEM((tm, d), BF16)],
        compiler_params=_params(2),
        name="ffn_final" if final_norm else "ffn",
    )(x, norm_g, w_in, w_in, w_out, final_g)


def _proj_kernel(h_ref, g_ref, w_ref, qkg_ref, cos_ref, sin_ref,
                 qat_ref, ka_ref, vat_ref, qbt_ref, kb_ref, vbt_ref):
    tm = h_ref.shape[0]
    u = _rms(h_ref[...], g_ref[...]).astype(BF16)
    cosf = cos_ref[...]
    sinf = sin_ref[...]
    lane = lax.broadcasted_iota(jnp.int32, (tm, HEAD), 1)
    even = (lane & 1) == 0

    def proj(c0, width):
        return jnp.dot(u, w_ref[:, c0:c0 + width], preferred_element_type=F32)

    def head(y, hh):
        return y[:, hh * HEAD:(hh + 1) * HEAD]

    def norm_rope(y, g):
        y = _rms(y, g)
        swapped = jnp.where(even, pltpu.roll(y, HEAD - 1, 1), pltpu.roll(y, 1, 1))
        return y * cosf + swapped * sinf

    chunk = 4 * HEAD
    for c in range(2):
        y = proj(c * chunk, chunk)
        for hh in range(4):
            qat_ref[c * 4 + hh] = (head(y, hh) * A_QSCALE).T.astype(BF16)
    for c in range(2):
        ka_ref[:, c * chunk:(c + 1) * chunk] = proj(A_COLS + c * chunk, chunk).astype(BF16)
    for c in range(2):
        y = proj(2 * A_COLS + c * chunk, chunk)
        for hh in range(4):
            vat_ref[c * 4 + hh, 0] = head(y, hh).T.astype(BF16)
    gq = qkg_ref[0:1, :]
    gk = qkg_ref[1:2, :]
    for c in range(2):
        y = proj(3 * A_COLS + c * chunk, chunk)
        for hh in range(4):
            qbt_ref[c * 4 + hh] = (norm_rope(head(y, hh), gq) * B_QSCALE).T.astype(BF16)
    y = proj(4 * A_COLS, chunk)
    for n in range(B_KV):
        kb_ref[:, n * HEAD:(n + 1) * HEAD] = norm_rope(head(y, n), gk).astype(BF16)
        vbt_ref[n, 0] = head(y, B_KV + n).T.astype(BF16)


def _proj(h, norm_g, w_in, qk_g, cosf, sinf, *, tm):
    n, d = h.shape
    nt = n // tm
    ntab = cosf.shape[0] // tm
    in_w = w_in.shape[1]
    out_shape = (
        jax.ShapeDtypeStruct((A_HEADS, HEAD, n), BF16),
        jax.ShapeDtypeStruct((n, A_COLS), BF16),
        jax.ShapeDtypeStruct((A_HEADS, nt, HEAD, tm), BF16),
        jax.ShapeDtypeStruct((B_HEADS, HEAD, n), BF16),
        jax.ShapeDtypeStruct((n, B_KV * HEAD), BF16),
        jax.ShapeDtypeStruct((B_KV, nt, HEAD, tm), BF16),
    )
    return pl.pallas_call(
        _proj_kernel,
        grid=(nt,),
        in_specs=[
            pl.BlockSpec((tm, d), lambda i: (i, 0)),
            pl.BlockSpec((1, d), lambda i: (0, 0)),
            pl.BlockSpec((d, in_w), lambda i: (0, 0), pipeline_mode=pl.Buffered(1)),
            pl.BlockSpec((2, HEAD), lambda i: (0, 0)),
            pl.BlockSpec((tm, HEAD), lambda i: (i % ntab, 0)),
            pl.BlockSpec((tm, HEAD), lambda i: (i % ntab, 0)),
        ],
        out_specs=(
            pl.BlockSpec((A_HEADS, HEAD, tm), lambda i: (0, 0, i)),
            pl.BlockSpec((tm, A_COLS), lambda i: (i, 0)),
            pl.BlockSpec((A_HEADS, 1, HEAD, tm), lambda i: (0, i, 0, 0)),
            pl.BlockSpec((B_HEADS, HEAD, tm), lambda i: (0, 0, i)),
            pl.BlockSpec((tm, B_KV * HEAD), lambda i: (i, 0)),
            pl.BlockSpec((B_KV, 1, HEAD, tm), lambda i: (0, i, 0, 0)),
        ),
        out_shape=out_shape,
        compiler_params=_params(1),
        name="mix_proj",
    )(h, norm_g, w_in, qk_g, cosf, sinf)


def _online_block(s, c, vt, m, l, acc_ref):
    m_new = jnp.maximum(m, jnp.max(s, axis=0, keepdims=True) + c)
    alpha = jnp.exp2(m - m_new)
    p = jnp.exp2(s - (m_new - c))
    l_new = alpha * l + jnp.sum(p, axis=0, keepdims=True)
    acc_ref[...] = alpha * acc_ref[...] + jnp.dot(vt, p.astype(BF16), preferred_element_type=F32)
    return m_new, l_new


def _first_block(s, vt, acc_ref):
    m = jnp.max(s, axis=0, keepdims=True)
    p = jnp.exp2(s - m)
    acc_ref[...] = jnp.dot(vt, p.astype(BF16), preferred_element_type=F32)
    return m, jnp.sum(p, axis=0, keepdims=True)


def _fixed_ref_blocks(lead, n_blocks, block_of, key_scores, off_of, vt_ref, acc_ref, p_buf):
    group = FIXED_REF_GROUP
    n_loop = max(n_blocks - 1, 0) // group
    first = n_blocks - n_loop * group
    assert group % 2 == 0 and len(lead) + first >= 2
    acc_started = []

    def values_of(prev):
        if callable(prev):
            return prev
        return lambda slot: jnp.dot(vt_ref[prev], p_buf[slot], preferred_element_type=F32)

    def add_values(prev, slot):
        prod = values_of(prev)(slot)
        if acc_started:
            acc_ref[...] += prod
        else:
            acc_ref[...] = prod
            acc_started.append(True)

    def block(slot, s, off, prev, l):
        if prev is not None:
            add_values(prev, 1 - slot)
        p = jnp.exp2(s - off)
        p_buf[slot, :s.shape[0]] = p.astype(BF16)
        return l + jnp.sum(p, axis=0, keepdims=True)

    l = jnp.zeros((1, acc_ref.shape[1]), F32)
    prev = None
    for n, (scores_fn, off, values_fn) in enumerate(lead):
        l = block(n % 2, scores_fn(), off, prev, l)
        prev = values_fn
    for u in range(first):
        j = block_of(u)
        l = block((len(lead) + u) % 2, key_scores(j), off_of(j), prev, l)
        prev = j

    def body(g, carry):
        l, prev = carry
        for u in range(group):
            j = block_of(first + g * group + u)
            l = block((len(lead) + first + u) % 2, key_scores(j), off_of(j), prev, l)
            prev = j
        return l, prev

    if n_loop:
        l, prev = lax.fori_loop(0, n_loop, body, (l, jnp.asarray(prev, jnp.int32)))
    add_values(prev, (len(lead) + n_blocks - 1) % 2)
    return l


def _key_value_bounds(k_ref, vt_ref, kmax_ref, room_ref):
    nk = vt_ref.shape[0]

    def kbody(kb, kmax):
        start = pl.multiple_of(kb * KEY_BLOCK, KEY_BLOCK)
        blk = jnp.abs(k_ref[pl.ds(start, KEY_BLOCK), :].astype(F32))
        return jnp.maximum(kmax, jnp.max(blk, axis=0, keepdims=True))
    kmax = lax.fori_loop(0, nk, kbody, jnp.zeros((1, HEAD), F32))
    kmax_ref[...] = jnp.broadcast_to(kmax, kmax_ref.shape).astype(BF16)

    def vbody(j, vmax):
        return jnp.maximum(vmax, jnp.max(jnp.abs(vt_ref[j].astype(F32)), axis=0, keepdims=True))
    vmax = lax.fori_loop(0, nk, vbody, jnp.ones((1, KEY_BLOCK), F32))
    vmax = jnp.max(vmax, axis=1, keepdims=True)
    room = MAX_EXPONENT - math.log2(nk * KEY_BLOCK) - jnp.log2(vmax)
    room_ref[...] = jnp.broadcast_to(room, room_ref.shape)


def _fixed_reference(kmax_ref, room_ref, qq, c_max):
    ub = jnp.dot(kmax_ref[...], jnp.abs(qq), preferred_element_type=F32)[0:1] * BOUND_SLACK
    return ub + c_max - room_ref[0:1, 0:1]


def _pipelined_blocks(qq, k_ref, vt_ref, c_of, m, l, acc_ref, s_buf, p_buf):
    nk = vt_ref.shape[0]
    kb_size = vt_ref.shape[2]
    assert nk % 2 == 0 and nk >= 2

    def stage_scores(j, slot, m):
        start = pl.multiple_of(j * kb_size, kb_size)
        s = jnp.dot(k_ref[pl.ds(start, kb_size), :], qq, preferred_element_type=F32)
        s_buf[slot] = s
        c = c_of(j)
        m_new = jnp.maximum(m, jnp.max(s, axis=0, keepdims=True) + c)
        return m_new, jnp.exp2(m - m_new), m_new - c

    def stage_exp(slot, off, alpha, l):
        p = jnp.exp2(s_buf[slot] - off)
        p_buf[slot] = p.astype(BF16)
        return alpha * l + jnp.sum(p, axis=0, keepdims=True)

    def stage_values(j, slot, alpha):
        acc_ref[...] = alpha * acc_ref[...] + jnp.dot(vt_ref[j], p_buf[slot],
                                                      preferred_element_type=F32)

    m, a0, o0 = stage_scores(0, 0, m)
    l = stage_exp(0, o0, a0, l)
    m, a1, o1 = stage_scores(1, 1, m)

    def pair(g, carry):
        m, l, a_pp, a_p, o_p = carry
        j = 2 * g
        stage_values(j - 2, 0, a_pp)
        l = stage_exp(1, o_p, a_p, l)
        m, a_j, o_j = stage_scores(j, 0, m)
        stage_values(j - 1, 1, a_p)
        l = stage_exp(0, o_j, a_j, l)
        m, a_j1, o_j1 = stage_scores(j + 1, 1, m)
        return m, l, a_j, a_j1, o_j1

    m, l, a_pp, a_p, o_p = lax.fori_loop(1, nk // 2, pair, (m, l, a0, a1, o1))
    stage_values(nk - 2, 0, a_pp)
    l = stage_exp(1, o_p, a_p, l)
    stage_values(nk - 1, 1, a_p)
    return m, l


def _attn_a_kernel(cfar_ref, lam_ref, g_ref, qt_ref, k_ref, vt_ref, km_ref, vmt_ref, bm_ref,
                   band_ref, o_ref, acc_ref, s_buf, p_buf, kmax_ref, room_ref):
    h = pl.program_id(1)
    i = pl.program_id(2)
    mq = qt_ref.shape[1]
    nk = vt_ref.shape[0]
    kb_size = vt_ref.shape[2]

    @pl.when(i == 0)
    def _():
        _key_value_bounds(k_ref, vt_ref, kmax_ref, room_ref)

    qt = qt_ref[...]
    row = lax.broadcasted_iota(jnp.int32, qt.shape, 0)
    zero = jnp.zeros_like(qt)
    qq = jnp.concatenate([jnp.where(row < A_QK, qt, zero), jnp.where(row >= A_QK, qt, zero)], axis=1)

    def scores(k):
        return jnp.dot(k, qq, preferred_element_type=F32)

    def twice(b):
        return jnp.concatenate([b, b], axis=1)

    def meta_scores():
        return scores(km_ref[...]) + twice(bm_ref[...])

    assert mq == kb_size and nk >= A_NEAR
    near = jnp.clip(i - 1, 0, nk - A_NEAR)

    def key_scores(kb):
        start = pl.multiple_of(kb * kb_size, kb_size)
        return scores(k_ref[pl.ds(start, kb_size), :])

    def band_block(n):
        kb = near + n
        return kb, lambda: key_scores(kb) + twice(band_ref[kb - i + A_NEAR - 1])

    c_lo = cfar_ref[3 * h]
    c_hi = cfar_ref[3 * h + 1]
    c_max = cfar_ref[3 * h + 2]

    m = _fixed_reference(kmax_ref, room_ref, qq, c_max)
    lead = [(meta_scores, m,
             lambda slot: jnp.dot(vmt_ref[...], p_buf[slot, :META_PAD], preferred_element_type=F32))]
    for n in range(A_NEAR):
        kb, scores_fn = band_block(n)
        lead.append((scores_fn, m, kb))
    l = _fixed_ref_blocks(lead, nk - A_NEAR, lambda jj: jj + jnp.where(jj >= near, A_NEAR, 0),
                          key_scores, lambda j: m - jnp.where(j < near, c_lo, c_hi),
                          vt_ref, acc_ref, p_buf)

    def running_reference(_):
        m, l = _first_block(meta_scores(), vmt_ref[...], acc_ref)
        for n in range(A_NEAR):
            kb, scores_fn = band_block(n)
            m, l = _online_block(scores_fn(), 0.0, vt_ref[kb], m, l, acc_ref)

        def c_of(j):
            return jnp.where(j < near, c_lo, jnp.where(j >= near + A_NEAR, c_hi, NEG))
        return _pipelined_blocks(qq, k_ref, vt_ref, c_of, m, l, acc_ref, s_buf, p_buf)[1]

    l = lax.cond(jnp.min(l) >= MIN_SUM, lambda l: l, running_reference, l)

    lp = lam_ref[...]
    lam = (jnp.exp(jnp.sum(lp[0:1] * lp[1:2], axis=1, keepdims=True))
           - jnp.exp(jnp.sum(lp[2:3] * lp[3:4], axis=1, keepdims=True)) + LAM_INIT)
    r = 1.0 / l
    acc = acc_ref[...]
    o = acc[:, :mq] * r[:, :mq] - lam * (acc[:, mq:] * r[:, mq:])
    o_ref[...] = (_rms(o.T, g_ref[...]) * (1.0 - LAM_INIT)).astype(BF16)


def _attn_scratch(width):
    return [pltpu.VMEM((HEAD, width), F32),
            pltpu.VMEM((2, KEY_BLOCK, width), F32),
            pltpu.VMEM((2, KEY_BLOCK, width), BF16),
            pltpu.VMEM((BOUND_ROWS, HEAD), BF16),
            pltpu.VMEM((8, HEAD), F32)]


def _attn_a(cfar, lam_p, subln_g, qat, ka, vat, kma, vmat, bias_meta, bands, *, batch):
    n = ka.shape[0]
    lr = n // batch
    nk = lr // KEY_BLOCK
    nq = lr // A_MQ
    return pl.pallas_call(
        _attn_a_kernel,
        grid=(batch, A_HEADS, nq),
        in_specs=[
            pl.BlockSpec(memory_space=pltpu.SMEM),
            pl.BlockSpec((4, A_QK), lambda b, h, i: (0, 0)),
            pl.BlockSpec((1, HEAD), lambda b, h, i: (0, 0)),
            pl.BlockSpec((None, HEAD, A_MQ), lambda b, h, i: (h, 0, b * nq + i)),
            pl.BlockSpec((lr, HEAD), lambda b, h, i: (b, h)),
            pl.BlockSpec((None, nk, HEAD, KEY_BLOCK), lambda b, h, i: (h, b, 0, 0)),
            pl.BlockSpec((META_PAD, HEAD), lambda b, h, i: (0, h)),
            pl.BlockSpec((None, None, HEAD, META_PAD), lambda b, h, i: (h, 0, 0, 0)),
            pl.BlockSpec((None, None, META_PAD, A_MQ), lambda b, h, i: (h, jnp.minimum(i, 1), 0, 0)),
            pl.BlockSpec((None, N_BANDS, KEY_BLOCK, A_MQ), lambda b, h, i: (h, 0, 0, 0)),
        ],
        out_specs=pl.BlockSpec((A_MQ, HEAD), lambda b, h, i: (b * nq + i, h)),
        out_shape=jax.ShapeDtypeStruct((n, A_COLS), BF16),
        scratch_shapes=_attn_scratch(2 * A_MQ),
        compiler_params=_params(3),
        name="attn_a",
    )(cfar, lam_p, subln_g, qat, ka, vat, kma, vmat, bias_meta, bands)


def _attn_b_kernel(qt_ref, k_ref, vt_ref, km_ref, vmt_ref, mask_ref, o_ref, acc_ref, s_buf, p_buf,
                   kmax_ref, room_ref):
    mq = qt_ref.shape[2]
    nk = vt_ref.shape[0]
    kb_size = vt_ref.shape[2]

    @pl.when(pl.program_id(2) == 0)
    def _():
        _key_value_bounds(k_ref, vt_ref, kmax_ref, room_ref)

    qq = jnp.concatenate([qt_ref[g] for g in range(B_GROUP)], axis=1)
    def meta_scores():
        return jnp.dot(km_ref[...], qq, preferred_element_type=F32) + mask_ref[...]

    def key_scores(j):
        start = pl.multiple_of(j * kb_size, kb_size)
        return jnp.dot(k_ref[pl.ds(start, kb_size), :], qq, preferred_element_type=F32)

    m = _fixed_reference(kmax_ref, room_ref, qq, 0.0)
    lead = [(meta_scores, m,
             lambda slot: jnp.dot(vmt_ref[...], p_buf[slot, :META_PAD], preferred_element_type=F32))]
    l = _fixed_ref_blocks(lead, nk, lambda jj: jj, key_scores, lambda j: m, vt_ref, acc_ref, p_buf)

    def running_reference(_):
        m, l = _first_block(meta_scores(), vmt_ref[...], acc_ref)
        return _pipelined_blocks(qq, k_ref, vt_ref, lambda j: 0.0, m, l, acc_ref, s_buf, p_buf)[1]

    l = lax.cond(jnp.min(l) >= MIN_SUM, lambda l: l, running_reference, l)
    o = acc_ref[...] * (1.0 / l)
    for g in range(B_GROUP):
        o_ref[:, g * HEAD:(g + 1) * HEAD] = o[:, g * mq:(g + 1) * mq].T.astype(BF16)


def _attn_b(qbt, kb, vbt, kmb, vmbt, mask, *, batch):
    n = kb.shape[0]
    lr = n // batch
    nk = lr // KEY_BLOCK
    nq = lr // B_MQ
    return pl.pallas_call(
        _attn_b_kernel,
        grid=(batch, B_KV, nq),
        in_specs=[
            pl.BlockSpec((B_GROUP, HEAD, B_MQ), lambda b, n_, i: (n_, 0, b * nq + i)),
            pl.BlockSpec((lr, HEAD), lambda b, n_, i: (b, n_)),
            pl.BlockSpec((None, nk, HEAD, KEY_BLOCK), lambda b, n_, i: (n_, b, 0, 0)),
            pl.BlockSpec((META_PAD, HEAD), lambda b, n_, i: (0, n_)),
            pl.BlockSpec((None, None, HEAD, META_PAD), lambda b, n_, i: (n_, 0, 0, 0)),
            pl.BlockSpec((META_PAD, B_GROUP * B_MQ), lambda b, n_, i: (0, 0)),
        ],
        out_specs=pl.BlockSpec((B_MQ, B_GROUP * HEAD), lambda b, n_, i: (b * nq + i, n_)),
        out_shape=jax.ShapeDtypeStruct((n, B_HEADS * HEAD), BF16),
        scratch_shapes=_attn_scratch(B_GROUP * B_MQ),
        compiler_params=_params(3),
        name="attn_b",
    )(qbt, kb, vbt, kmb, vmbt, mask)


def _out_kernel(h_ref, ya_ref, yb_ref, w_ref, o_ref):
    o_ref[...] = (h_ref[...]
                  + jnp.dot(ya_ref[...], w_ref[:A_COLS, :], preferred_element_type=F32)
                  + jnp.dot(yb_ref[...], w_ref[A_COLS:, :], preferred_element_type=F32))


def _out_proj(h, ya, yb, w_out, *, tm):
    n, d = h.shape
    return pl.pallas_call(
        _out_kernel,
        grid=(n // tm,),
        in_specs=[
            pl.BlockSpec((tm, d), lambda i: (i, 0)),
            pl.BlockSpec((tm, A_COLS), lambda i: (i, 0)),
            pl.BlockSpec((tm, B_HEADS * HEAD), lambda i: (i, 0)),
            pl.BlockSpec(w_out.shape, lambda i: (0, 0), pipeline_mode=pl.Buffered(1)),
        ],
        out_specs=pl.BlockSpec((tm, d), lambda i: (i, 0)),
        out_shape=jax.ShapeDtypeStruct((n, d), F32),
        compiler_params=_params(1),
        name="out_proj",
    )(h, ya, yb, w_out)


def _t5_bucket(rel):
    half = REL_BUCKETS // 2
    max_exact = half // 2
    n = jnp.abs(rel)
    sign_off = jnp.where(rel > 0, half, 0)
    nf = jnp.maximum(n, 1).astype(F32)
    large = max_exact + (jnp.log(nf / max_exact) / math.log(REL_MAX_DIST / max_exact)
                         * (half - max_exact)).astype(jnp.int32)
    large = jnp.minimum(large, half - 1)
    return sign_off + jnp.where(n < max_exact, n, large)


def _toeplitz_kernel(vec_ref, o_ref):
    rows, cols = o_ref.shape
    x = jnp.broadcast_to(vec_ref[...], (rows, vec_ref.shape[1]))
    o_ref[...] = pltpu.roll(x, 0, 1, stride=1, stride_axis=0)[:, :cols]


def _toeplitz(f, rows, cols):
    period = rows + cols
    x = jnp.arange(period)
    vec = f(jnp.where(x < cols, -x, period - x))
    n = vec.shape[0]
    return pl.pallas_call(
        _toeplitz_kernel,
        grid=(n,),
        in_specs=[pl.BlockSpec((None, 1, period), lambda i: (i, 0, 0))],
        out_specs=pl.BlockSpec((None, rows, cols), lambda i: (i, 0, 0)),
        out_shape=jax.ShapeDtypeStruct((n, rows, cols), F32),
        compiler_params=_params(1),
        name="bias_bands",
    )(vec[:, None, :])


def _bias_tables(rel_table):
    rel1d = jnp.arange(-REL_CLIP, REL_CLIP + 1)
    t1d = rel_table.astype(F32)[_t5_bucket(rel1d)].T * LOG2E

    def lookup(rel):
        return t1d[:, jnp.clip(rel, -REL_CLIP, REL_CLIP) + REL_CLIP]

    delta = ((jnp.arange(N_BANDS) - (A_NEAR - 1)) * KEY_BLOCK)[:, None]
    bands = _toeplitz(lambda y: lookup(delta + y[None, :]).reshape(A_HEADS * N_BANDS, -1),
                      KEY_BLOCK, A_MQ).reshape(A_HEADS, N_BANDS, KEY_BLOCK, A_MQ)

    c_lo = t1d[:, 0]
    c_hi = t1d[:, -1]
    cfar = jnp.stack([c_lo, c_hi, jnp.max(t1d, axis=1)], axis=1).reshape(-1)

    j = jnp.arange(META_PAD)[:, None]
    first = _toeplitz(lambda y: lookup(y - N_META), META_PAD, A_MQ)
    rest = jnp.broadcast_to(c_lo[:, None, None], first.shape)
    bias_meta = jnp.where((j < N_META)[None], jnp.stack([first, rest], axis=1), NEG)
    return bands, cfar, bias_meta


def _rope_tables(lr):
    rows = lr // GRID_W
    row = jnp.repeat(jnp.arange(rows), GRID_W).astype(F32)
    col = jnp.tile(jnp.arange(GRID_W), rows).astype(F32)
    axis_dim = HEAD // 2
    freqs = ROPE_THETA ** (-jnp.arange(0, axis_dim, 2, dtype=F32) / axis_dim)
    ang = jnp.concatenate([row[:, None] * freqs, col[:, None] * freqs], axis=-1)
    cos, sin = jnp.cos(ang), jnp.sin(ang)
    cosf = jnp.repeat(cos, 2, axis=-1)
    sinf = jnp.stack([-sin, sin], axis=-1).reshape(lr, HEAD)
    return cosf, sinf


def kernel(x_prompt, x_sample, meta_tokens, rel_bias_table, ffn1_norm, ffn1_w_in, ffn1_w_out,
           mix_norm, w_in, diff_lambda, diff_subln, qk_norm, w_out,
           ffn2_norm, ffn2_w_in, ffn2_w_out, final_norm):
    w1i, w1o = ffn1_w_in[0].astype(BF16), ffn1_w_out[0].astype(BF16)
    w2i, w2o = ffn2_w_in[0].astype(BF16), ffn2_w_out[0].astype(BF16)
    wi, wo = w_in[0].astype(BF16), w_out[0].astype(BF16)
    g1, gm, g2 = ffn1_norm, mix_norm, ffn2_norm
    gf = final_norm[None, :]
    qkg = qk_norm[0]
    lam_p = diff_lambda[0]
    subln = diff_subln

    bands, cfar, bias_meta = _bias_tables(rel_bias_table)
    mask_b = jnp.where(jnp.arange(META_PAD)[:, None] < N_META, 0.0, NEG).astype(F32)
    mask_b = jnp.broadcast_to(mask_b, (META_PAD, B_GROUP * B_MQ))

    xm = jnp.zeros((META_PAD, D_MODEL), F32).at[:N_META].set(meta_tokens)
    hm = _ffn(xm, g1, w1i, w1o, gf, tm=META_PAD, final_norm=False)
    ones = jnp.ones((META_PAD, HEAD), F32)
    _, kma, vmat, _, kmb, vmbt = _proj(hm, gm, wi, qkg, ones, jnp.zeros_like(ones), tm=META_PAD)

    def trunk(x):
        batch, lr, d = x.shape
        h = _ffn(x.reshape(batch * lr, d), g1, w1i, w1o, gf, tm=FFN_TM, final_norm=False)
        cosf, sinf = _rope_tables(lr)
        qat, ka, vat, qbt, kb, vbt = _proj(h, gm, wi, qkg, cosf, sinf, tm=PROJ_TM)
        ya = _attn_a(cfar, lam_p, subln, qat, ka, vat, kma, vmat, bias_meta, bands, batch=batch)
        yb = _attn_b(qbt, kb, vbt, kmb, vmbt, mask_b, batch=batch)
        h = _out_proj(h, ya, yb, wo, tm=PROJ_TM)
        y = _ffn(h, g2, w2i, w2o, gf, tm=FFN_TM, final_norm=True)
        return y.reshape(batch, lr, d)

    return trunk(x_prompt), trunk(x_sample)
```

```python
import functools
import math

import jax
import jax.numpy as jnp
from jax import lax
from jax.experimental import pallas as pl
from jax.experimental.pallas import tpu as pltpu

F32 = jnp.float32
BF16 = jnp.bfloat16

EPS = 1e-6
N_META = 16
GRID_W = 64
D_MODEL = 2048
A_HEADS = 8
A_QK = 64
HEAD = 128
B_HEADS = 8
B_KV = 2
B_GROUP = B_HEADS // B_KV
A_COLS = A_HEADS * HEAD
REL_BUCKETS = 32
REL_MAX_DIST = 128
ROPE_THETA = 10000.0
LAM_INIT = 0.8 - 0.6 * math.exp(-0.3 * 0)
LOG2E = math.log2(math.e)
A_QSCALE = A_QK ** -0.5 * LOG2E
B_QSCALE = HEAD ** -0.5 * LOG2E
NEG = -1e30

META_PAD = 128
REL_CLIP = 128

VMEM_LIMIT = 60 * 1024 * 1024

FFN_TM = 1024
FFN_TF = 512
FFN_TN = 512
PROJ_TM = 512
KEY_BLOCK = PROJ_TM
A_MQ = 512
B_MQ = 256
A_NEAR = 4
N_BANDS = 2 * A_NEAR - 1
MAX_EXPONENT = 126.0
BOUND_ROWS = 16
BOUND_SLACK = 1.01
MIN_SUM = 2.0 ** -60
FIXED_REF_GROUP = 16


def _rms(x, g):
    return x * lax.rsqrt(jnp.mean(x * x, axis=-1, keepdims=True) + EPS) * g


def _params(n_grid_dims):
    return pltpu.CompilerParams(
        dimension_semantics=("arbitrary",) * n_grid_dims,
        vmem_limit_bytes=VMEM_LIMIT,
    )


def _ffn_kernel(x_ref, g_ref, wg_ref, wu_ref, wo_ref, fg_ref, o_ref, xn_ref, *, final_norm):
    j = pl.program_id(1)

    @pl.when(j == 0)
    def _():
        x = x_ref[...]
        xn_ref[...] = _rms(x, g_ref[...]).astype(BF16)
        o_ref[...] = x

    xn = xn_ref[...]
    gate = jnp.dot(xn, wg_ref[...], preferred_element_type=F32)
    up = jnp.dot(xn, wu_ref[...], preferred_element_type=F32)
    act = (0.5 * gate / (1.0 + jnp.exp(-gate)) * up).astype(BF16)
    d = o_ref.shape[1]
    for n0 in range(0, d, FFN_TN):
        o_ref[:, n0:n0 + FFN_TN] += jnp.dot(act, wo_ref[:, n0:n0 + FFN_TN],
                                            preferred_element_type=F32)

    if final_norm:
        @pl.when(j == pl.num_programs(1) - 1)
        def _():
            o_ref[...] = _rms(o_ref[...], fg_ref[...])


def _ffn(x, norm_g, w_in, w_out, final_g, *, tm, final_norm):
    n, d = x.shape
    dff = w_out.shape[0]
    nj = dff // FFN_TF
    return pl.pallas_call(
        functools.partial(_ffn_kernel, final_norm=final_norm),
        grid=(n // tm, nj),
        in_specs=[
            pl.BlockSpec((tm, d), lambda i, j: (i, 0)),
            pl.BlockSpec((1, d), lambda i, j: (0, 0)),
            pl.BlockSpec((d, FFN_TF), lambda i, j: (0, j)),
            pl.BlockSpec((d, FFN_TF), lambda i, j: (0, j + nj)),
            pl.BlockSpec((FFN_TF, d), lambda i, j: (j, 0)),
            pl.BlockSpec((1, d), lambda i, j: (0, 0)),
        ],
        out_specs=pl.BlockSpec((tm, d), lambda i, j: (i, 0)),
        out_shape=jax.ShapeDtypeStruct((n, d), F32),
        scratch_shapes=[pltpu.VMEM((tm, d), BF16)],
        compiler_params=_params(2),
        name="ffn_final" if final_norm else "ffn",
    )(x, norm_g, w_in, w_in, w_out, final_g)


def _proj_kernel(h_ref, g_ref, w_ref, qkg_ref, cos_ref, sin_ref,
                 qat_ref, ka_ref, vat_ref, qbt_ref, kb_ref, vbt_ref):
    tm = h_ref.shape[0]
    u = _rms(h_ref[...], g_ref[...]).astype(BF16)
    cosf = cos_ref[...]
    sinf = sin_ref[...]
    lane = lax.broadcasted_iota(jnp.int32, (tm, HEAD), 1)
    even = (lane & 1) == 0

    def proj(c0, width):
        return jnp.dot(u, w_ref[:, c0:c0 + width], preferred_element_type=F32)

    def head(y, hh):
        return y[:, hh * HEAD:(hh + 1) * HEAD]

    def norm_rope(y, g):
        y = _rms(y, g)
        swapped = jnp.where(even, pltpu.roll(y, HEAD - 1, 1), pltpu.roll(y, 1, 1))
        return y * cosf + swapped * sinf

    chunk = 4 * HEAD
    for c in range(2):
        y = proj(c * chunk, chunk)
        for hh in range(4):
            qat_ref[c * 4 + hh] = (head(y, hh) * A_QSCALE).T.astype(BF16)
    for c in range(2):
        ka_ref[:, c * chunk:(c + 1) * chunk] = proj(A_COLS + c * chunk, chunk).astype(BF16)
    for c in range(2):
        y = proj(2 * A_COLS + c * chunk, chunk)
        for hh in range(4):
            vat_ref[c * 4 + hh, 0] = head(y, hh).T.astype(BF16)
    gq = qkg_ref[0:1, :]
    gk = qkg_ref[1:2, :]
    for c in range(2):
        y = proj(3 * A_COLS + c * chunk, chunk)
        for hh in range(4):
            qbt_ref[c * 4 + hh] = (norm_rope(head(y, hh), gq) * B_QSCALE).T.astype(BF16)
    y = proj(4 * A_COLS, chunk)
    for n in range(B_KV):
        kb_ref[:, n * HEAD:(n + 1) * HEAD] = norm_rope(head(y, n), gk).astype(BF16)
        vbt_ref[n, 0] = head(y, B_KV + n).T.astype(BF16)


def _proj(h, norm_g, w_in, qk_g, cosf, sinf, *, tm):
    n, d = h.shape
    nt = n // tm
    ntab = cosf.shape[0] // tm
    in_w = w_in.shape[1]
    out_shape = (
        jax.ShapeDtypeStruct((A_HEADS, HEAD, n), BF16),
        jax.ShapeDtypeStruct((n, A_COLS), BF16),
        jax.ShapeDtypeStruct((A_HEADS, nt, HEAD, tm), BF16),
        jax.ShapeDtypeStruct((B_HEADS, HEAD, n), BF16),
        jax.ShapeDtypeStruct((n, B_KV * HEAD), BF16),
        jax.ShapeDtypeStruct((B_KV, nt, HEAD, tm), BF16),
    )
    return pl.pallas_call(
        _proj_kernel,
        grid=(nt,),
        in_specs=[
            pl.BlockSpec((tm, d), lambda i: (i, 0)),
            pl.BlockSpec((1, d), lambda i: (0, 0)),
            pl.BlockSpec((d, in_w), lambda i: (0, 0), pipeline_mode=pl.Buffered(1)),
            pl.BlockSpec((2, HEAD), lambda i: (0, 0)),
            pl.BlockSpec((tm, HEAD), lambda i: (i % ntab, 0)),
            pl.BlockSpec((tm, HEAD), lambda i: (i % ntab, 0)),
        ],
        out_specs=(
            pl.BlockSpec((A_HEADS, HEAD, tm), lambda i: (0, 0, i)),
            pl.BlockSpec((tm, A_COLS), lambda i: (i, 0)),
            pl.BlockSpec((A_HEADS, 1, HEAD, tm), lambda i: (0, i, 0, 0)),
            pl.BlockSpec((B_HEADS, HEAD, tm), lambda i: (0, 0, i)),
            pl.BlockSpec((tm, B_KV * HEAD), lambda i: (i, 0)),
            pl.BlockSpec((B_KV, 1, HEAD, tm), lambda i: (0, i, 0, 0)),
        ),
        out_shape=out_shape,
        compiler_params=_params(1),
        name="mix_proj",
    )(h, norm_g, w_in, qk_g, cosf, sinf)


def _online_block(s, c, vt, m, l, acc_ref):
    m_new = jnp.maximum(m, jnp.max(s, axis=0, keepdims=True) + c)
    alpha = jnp.exp2(m - m_new)
    p = jnp.exp2(s - (m_new - c))
    l_new = alpha * l + jnp.sum(p, axis=0, keepdims=True)
    acc_ref[...] = alpha * acc_ref[...] + jnp.dot(vt, p.astype(BF16), preferred_element_type=F32)
    return m_new, l_new


def _first_block(s, vt, acc_ref):
    m = jnp.max(s, axis=0, keepdims=True)
    p = jnp.exp2(s - m)
    acc_ref[...] = jnp.dot(vt, p.astype(BF16), preferred_element_type=F32)
    return m, jnp.sum(p, axis=0, keepdims=True)


def _fixed_ref_blocks(lead, n_blocks, block_of, key_scores, off_of, vt_ref, acc_ref, p_buf):
    group = FIXED_REF_GROUP
    n_loop = max(n_blocks - 1, 0) // group
    first = n_blocks - n_loop * group
    assert group % 2 == 0 and len(lead) + first >= 2
    acc_started = []

    def values_of(prev):
        if callable(prev):
            return prev
        return lambda slot: jnp.dot(vt_ref[prev], p_buf[slot], preferred_element_type=F32)

    def add_values(prev, slot):
        prod = values_of(prev)(slot)
        if acc_started:
            acc_ref[...] += prod
        else:
            acc_ref[...] = prod
            acc_started.append(True)

    def block(slot, s, off, prev, l):
        if prev is not None:
            add_values(prev, 1 - slot)
        p = jnp.exp2(s - off)
        p_buf[slot, :s.shape[0]] = p.astype(BF16)
        return l + jnp.sum(p, axis=0, keepdims=True)

    l = jnp.zeros((1, acc_ref.shape[1]), F32)
    prev = None
    for n, (scores_fn, off, values_fn) in enumerate(lead):
        l = block(n % 2, scores_fn(), off, prev, l)
        prev = values_fn
    for u in range(first):
        j = block_of(u)
        l = block((len(lead) + u) % 2, key_scores(j), off_of(j), prev, l)
        prev = j

    def body(g, carry):
        l, prev = carry
        for u in range(group):
            j = block_of(first + g * group + u)
            l = block((len(lead) + first + u) % 2, key_scores(j), off_of(j), prev, l)
            prev = j
        return l, prev

    if n_loop:
        l, prev = lax.fori_loop(0, n_loop, body, (l, jnp.asarray(prev, jnp.int32)))
    add_values(prev, (len(lead) + n_blocks - 1) % 2)
    return l


def _key_value_bounds(k_ref, vt_ref, km_ref, vmt_ref, kmax_ref, room_ref):
    nk = vt_ref.shape[0]

    def kbody(kb, kmax):
        start = pl.multiple_of(kb * KEY_BLOCK, KEY_BLOCK)
        blk = jnp.abs(k_ref[pl.ds(start, KEY_BLOCK), :].astype(F32))
        return jnp.maximum(kmax, jnp.max(blk, axis=0, keepdims=True))
    kmax = jnp.max(jnp.abs(km_ref[...].astype(F32)), axis=0, keepdims=True)
    kmax = lax.fori_loop(0, nk, kbody, kmax)
    kmax_ref[...] = jnp.broadcast_to(kmax, kmax_ref.shape).astype(BF16)

    def vbody(j, vmax):
        return jnp.maximum(vmax, jnp.max(jnp.abs(vt_ref[j].astype(F32)), axis=0, keepdims=True))
    vmax = lax.fori_loop(0, nk, vbody, jnp.ones((1, KEY_BLOCK), F32))
    vmeta = jnp.max(jnp.abs(vmt_ref[...].astype(F32)), axis=0, keepdims=True)
    vmax = jnp.maximum(jnp.max(vmax, axis=1, keepdims=True),
                       jnp.max(vmeta, axis=1, keepdims=True))
    room = MAX_EXPONENT - math.log2(nk * KEY_BLOCK + META_PAD) - jnp.log2(vmax)
    room_ref[...] = jnp.broadcast_to(room, room_ref.shape)


def _fixed_reference(kmax_ref, room_ref, qq, c_max):
    ub = jnp.dot(kmax_ref[...], jnp.abs(qq), preferred_element_type=F32)[0:1] * BOUND_SLACK
    return ub + c_max - room_ref[0:1, 0:1]


def _pipelined_blocks(qq, k_ref, vt_ref, c_of, m, l, acc_ref, s_buf, p_buf):
    nk = vt_ref.shape[0]
    kb_size = vt_ref.shape[2]
    assert nk % 2 == 0 and nk >= 2

    def stage_scores(j, slot, m):
        start = pl.multiple_of(j * kb_size, kb_size)
        s = jnp.dot(k_ref[pl.ds(start, kb_size), :], qq, preferred_element_type=F32)
        s_buf[slot] = s
        c = c_of(j)
        m_new = jnp.maximum(m, jnp.max(s, axis=0, keepdims=True) + c)
        return m_new, jnp.exp2(m - m_new), m_new - c

    def stage_exp(slot, off, alpha, l):
        p = jnp.exp2(s_buf[slot] - off)
        p_buf[slot] = p.astype(BF16)
        return alpha * l + jnp.sum(p, axis=0, keepdims=True)

    def stage_values(j, slot, alpha):
        acc_ref[...] = alpha * acc_ref[...] + jnp.dot(vt_ref[j], p_buf[slot],
                                                      preferred_element_type=F32)

    m, a0, o0 = stage_scores(0, 0, m)
    l = stage_exp(0, o0, a0, l)
    m, a1, o1 = stage_scores(1, 1, m)

    def pair(g, carry):
        m, l, a_pp, a_p, o_p = carry
        j = 2 * g
        stage_values(j - 2, 0, a_pp)
        l = stage_exp(1, o_p, a_p, l)
        m, a_j, o_j = stage_scores(j, 0, m)
        stage_values(j - 1, 1, a_p)
        l = stage_exp(0, o_j, a_j, l)
        m, a_j1, o_j1 = stage_scores(j + 1, 1, m)
        return m, l, a_j, a_j1, o_j1

    m, l, a_pp, a_p, o_p = lax.fori_loop(1, nk // 2, pair, (m, l, a0, a1, o1))
    stage_values(nk - 2, 0, a_pp)
    l = stage_exp(1, o_p, a_p, l)
    stage_values(nk - 1, 1, a_p)
    return m, l


def _attn_a_kernel(cfar_ref, lam_ref, g_ref, qt_ref, k_ref, vt_ref, km_ref, vmt_ref, bm_ref,
                   band_ref, o_ref, acc_ref, s_buf, p_buf, kmax_ref, room_ref):
    h = pl.program_id(1)
    i = pl.program_id(2)
    mq = qt_ref.shape[1]
    nk = vt_ref.shape[0]
    kb_size = vt_ref.shape[2]

    @pl.when(i == 0)
    def _():
        _key_value_bounds(k_ref, vt_ref, km_ref, vmt_ref, kmax_ref, room_ref)

    qt = qt_ref[...]
    row = lax.broadcasted_iota(jnp.int32, qt.shape, 0)
    zero = jnp.zeros_like(qt)
    qq = jnp.concatenate([jnp.where(row < A_QK, qt, zero), jnp.where(row >= A_QK, qt, zero)], axis=1)

    lp = lam_ref[...]
    lam = (jnp.exp(jnp.sum(lp[0:1] * lp[1:2], axis=1, keepdims=True))
           - jnp.exp(jnp.sum(lp[2:3] * lp[3:4], axis=1, keepdims=True)) + LAM_INIT)

    def scores(k):
        return jnp.dot(k, qq, preferred_element_type=F32)

    def twice(b):
        return jnp.concatenate([b, b], axis=1)

    def meta_scores():
        return scores(km_ref[...]) + twice(bm_ref[...])

    assert mq == kb_size and nk >= A_NEAR
    near = jnp.clip(i - 1, 0, nk - A_NEAR)

    def key_scores(kb):
        start = pl.multiple_of(kb * kb_size, kb_size)
        return scores(k_ref[pl.ds(start, kb_size), :])

    def band_block(n):
        kb = near + n
        return kb, lambda: key_scores(kb) + twice(band_ref[kb - i + A_NEAR - 1])

    c_lo = cfar_ref[3 * h]
    c_hi = cfar_ref[3 * h + 1]
    c_max = cfar_ref[3 * h + 2]

    m = _fixed_reference(kmax_ref, room_ref, qq, c_max)
    lead = [(meta_scores, m,
             lambda slot: jnp.dot(vmt_ref[...], p_buf[slot, :META_PAD], preferred_element_type=F32))]
    for n in range(A_NEAR):
        kb, scores_fn = band_block(n)
        lead.append((scores_fn, m, kb))
    l = _fixed_ref_blocks(lead, nk - A_NEAR, lambda jj: jj + jnp.where(jj >= near, A_NEAR, 0),
                          key_scores, lambda j: m - jnp.where(j < near, c_lo, c_hi),
                          vt_ref, acc_ref, p_buf)

    def running_reference(_):
        m, l = _first_block(meta_scores(), vmt_ref[...], acc_ref)
        for n in range(A_NEAR):
            kb, scores_fn = band_block(n)
            m, l = _online_block(scores_fn(), 0.0, vt_ref[kb], m, l, acc_ref)

        def c_of(j):
            return jnp.where(j < near, c_lo, jnp.where(j >= near + A_NEAR, c_hi, NEG))
        return _pipelined_blocks(qq, k_ref, vt_ref, c_of, m, l, acc_ref, s_buf, p_buf)[1]

    l = lax.cond(jnp.min(l) >= MIN_SUM, lambda l: l, running_reference, l)

    r = 1.0 / l
    acc = acc_ref[...]
    o = acc[:, :mq] * r[:, :mq] - lam * (acc[:, mq:] * r[:, mq:])
    o_ref[...] = (_rms(o.T, g_ref[...]) * (1.0 - LAM_INIT)).astype(BF16)


def _attn_scratch(width):
    return [pltpu.VMEM((HEAD, width), F32),
            pltpu.VMEM((2, KEY_BLOCK, width), F32),
            pltpu.VMEM((2, KEY_BLOCK, width), BF16),
            pltpu.VMEM((BOUND_ROWS, HEAD), BF16),
            pltpu.VMEM((8, HEAD), F32)]


def _attn_a(cfar, lam_p, subln_g, qat, ka, vat, kma, vmat, bias_meta, bands, *, batch):
    n = ka.shape[0]
    lr = n // batch
    nk = lr // KEY_BLOCK
    nq = lr // A_MQ
    return pl.pallas_call(
        _attn_a_kernel,
        grid=(batch, A_HEADS, nq),
        in_specs=[
            pl.BlockSpec(memory_space=pltpu.SMEM),
            pl.BlockSpec((4, A_QK), lambda b, h, i: (0, 0)),
            pl.BlockSpec((1, HEAD), lambda b, h, i: (0, 0)),
            pl.BlockSpec((None, HEAD, A_MQ), lambda b, h, i: (h, 0, b * nq + i)),
            pl.BlockSpec((lr, HEAD), lambda b, h, i: (b, h)),
            pl.BlockSpec((None, nk, HEAD, KEY_BLOCK), lambda b, h, i: (h, b, 0, 0)),
            pl.BlockSpec((META_PAD, HEAD), lambda b, h, i: (0, h)),
            pl.BlockSpec((None, None, HEAD, META_PAD), lambda b, h, i: (h, 0, 0, 0)),
            pl.BlockSpec((None, None, META_PAD, A_MQ), lambda b, h, i: (h, jnp.minimum(i, 1), 0, 0)),
            pl.BlockSpec((None, N_BANDS, KEY_BLOCK, A_MQ), lambda b, h, i: (h, 0, 0, 0)),
        ],
        out_specs=pl.BlockSpec((A_MQ, HEAD), lambda b, h, i: (b * nq + i, h)),
        out_shape=jax.ShapeDtypeStruct((n, A_COLS), BF16),
        scratch_shapes=_attn_scratch(2 * A_MQ),
        compiler_params=_params(3),
        name="attn_a",
    )(cfar, lam_p, subln_g, qat, ka, vat, kma, vmat, bias_meta, bands)


def _attn_b_kernel(qt_ref, k_ref, vt_ref, km_ref, vmt_ref, mask_ref, o_ref, acc_ref, s_buf, p_buf,
                   kmax_ref, room_ref):
    mq = qt_ref.shape[2]
    nk = vt_ref.shape[0]
    kb_size = vt_ref.shape[2]

    @pl.when(pl.program_id(2) == 0)
    def _():
        _key_value_bounds(k_ref, vt_ref, km_ref, vmt_ref, kmax_ref, room_ref)

    qq = jnp.concatenate([qt_ref[g] for g in range(B_GROUP)], axis=1)
    def meta_scores():
        return jnp.dot(km_ref[...], qq, preferred_element_type=F32) + mask_ref[...]

    def key_scores(j):
        start = pl.multiple_of(j * kb_size, kb_size)
        return jnp.dot(k_ref[pl.ds(start, kb_size), :], qq, preferred_element_type=F32)

    m = _fixed_reference(kmax_ref, room_ref, qq, 0.0)
    lead = [(meta_scores, m,
             lambda slot: jnp.dot(vmt_ref[...], p_buf[slot, :META_PAD], preferred_element_type=F32))]
    l = _fixed_ref_blocks(lead, nk, lambda jj: jj, key_scores, lambda j: m, vt_ref, acc_ref, p_buf)

    def running_reference(_):
        m, l = _first_block(meta_scores(), vmt_ref[...], acc_ref)
        return _pipelined_blocks(qq, k_ref, vt_ref, lambda j: 0.0, m, l, acc_ref, s_buf, p_buf)[1]

    l = lax.cond(jnp.min(l) >= MIN_SUM, lambda l: l, running_reference, l)
    o = acc_ref[...] * (1.0 / l)
    for g in range(B_GROUP):
        o_ref[:, g * HEAD:(g + 1) * HEAD] = o[:, g * mq:(g + 1) * mq].T.astype(BF16)


def _attn_b(qbt, kb, vbt, kmb, vmbt, mask, *, batch):
    n = kb.shape[0]
    lr = n // batch
    nk = lr // KEY_BLOCK
    nq = lr // B_MQ
    return pl.pallas_call(
        _attn_b_kernel,
        grid=(batch, B_KV, nq),
        in_specs=[
            pl.BlockSpec((B_GROUP, HEAD, B_MQ), lambda b, n_, i: (n_, 0, b * nq + i)),
            pl.BlockSpec((lr, HEAD), lambda b, n_, i: (b, n_)),
            pl.BlockSpec((None, nk, HEAD, KEY_BLOCK), lambda b, n_, i: (n_, b, 0, 0)),
            pl.BlockSpec((META_PAD, HEAD), lambda b, n_, i: (0, n_)),
            pl.BlockSpec((None, None, HEAD, META_PAD), lambda b, n_, i: (n_, 0, 0, 0)),
            pl.BlockSpec((META_PAD, B_GROUP * B_MQ), lambda b, n_, i: (0, 0)),
        ],
        out_specs=pl.BlockSpec((B_MQ, B_GROUP * HEAD), lambda b, n_, i: (b * nq + i, n_)),
        out_shape=jax.ShapeDtypeStruct((n, B_HEADS * HEAD), BF16),
        scratch_shapes=_attn_scratch(B_GROUP * B_MQ),
        compiler_params=_params(3),
        name="attn_b",
    )(qbt, kb, vbt, kmb, vmbt, mask)


def _out_kernel(h_ref, ya_ref, yb_ref, w_ref, o_ref):
    o_ref[...] = (h_ref[...]
                  + jnp.dot(ya_ref[...], w_ref[:A_COLS, :], preferred_element_type=F32)
                  + jnp.dot(yb_ref[...], w_ref[A_COLS:, :], preferred_element_type=F32))


def _out_proj(h, ya, yb, w_out, *, tm):
    n, d = h.shape
    return pl.pallas_call(
        _out_kernel,
        grid=(n // tm,),
        in_specs=[
            pl.BlockSpec((tm, d), lambda i: (i, 0)),
            pl.BlockSpec((tm, A_COLS), lambda i: (i, 0)),
            pl.BlockSpec((tm, B_HEADS * HEAD), lambda i: (i, 0)),
            pl.BlockSpec(w_out.shape, lambda i: (0, 0), pipeline_mode=pl.Buffered(1)),
        ],
        out_specs=pl.BlockSpec((tm, d), lambda i: (i, 0)),
        out_shape=jax.ShapeDtypeStruct((n, d), F32),
        compiler_params=_params(1),
        name="out_proj",
    )(h, ya, yb, w_out)


def _t5_bucket(rel):
    half = REL_BUCKETS // 2
    max_exact = half // 2
    n = jnp.abs(rel)
    sign_off = jnp.where(rel > 0, half, 0)
    nf = jnp.maximum(n, 1).astype(F32)
    large = max_exact + (jnp.log(nf / max_exact) / math.log(REL_MAX_DIST / max_exact)
                         * (half - max_exact)).astype(jnp.int32)
    large = jnp.minimum(large, half - 1)
    return sign_off + jnp.where(n < max_exact, n, large)


def _toeplitz_kernel(vec_ref, o_ref):
    rows, cols = o_ref.shape
    x = jnp.broadcast_to(vec_ref[...], (rows, vec_ref.shape[1]))
    o_ref[...] = pltpu.roll(x, 0, 1, stride=1, stride_axis=0)[:, :cols]


def _toeplitz(f, rows, cols):
    period = rows + cols
    x = jnp.arange(period)
    vec = f(jnp.where(x < cols, -x, period - x))
    n = vec.shape[0]
    return pl.pallas_call(
        _toeplitz_kernel,
        grid=(n,),
        in_specs=[pl.BlockSpec((None, 1, period), lambda i: (i, 0, 0))],
        out_specs=pl.BlockSpec((None, rows, cols), lambda i: (i, 0, 0)),
        out_shape=jax.ShapeDtypeStruct((n, rows, cols), F32),
        compiler_params=_params(1),
        name="bias_bands",
    )(vec[:, None, :])


def _bias_tables(rel_table):
    rel1d = jnp.arange(-REL_CLIP, REL_CLIP + 1)
    t1d = rel_table.astype(F32)[_t5_bucket(rel1d)].T * LOG2E

    def lookup(rel):
        return t1d[:, jnp.clip(rel, -REL_CLIP, REL_CLIP) + REL_CLIP]

    delta = ((jnp.arange(N_BANDS) - (A_NEAR - 1)) * KEY_BLOCK)[:, None]
    bands = _toeplitz(lambda y: lookup(delta + y[None, :]).reshape(A_HEADS * N_BANDS, -1),
                      KEY_BLOCK, A_MQ).reshape(A_HEADS, N_BANDS, KEY_BLOCK, A_MQ)

    c_lo = t1d[:, 0]
    c_hi = t1d[:, -1]
    cfar = jnp.stack([c_lo, c_hi, jnp.max(t1d, axis=1)], axis=1).reshape(-1)

    j = jnp.arange(META_PAD)[:, None]
    first = _toeplitz(lambda y: lookup(y - N_META), META_PAD, A_MQ)
    rest = jnp.broadcast_to(c_lo[:, None, None], first.shape)
    bias_meta = jnp.where((j < N_META)[None], jnp.stack([first, rest], axis=1), NEG)
    return bands, cfar, bias_meta


def _rope_tables(lr):
    rows = lr // GRID_W
    row = jnp.repeat(jnp.arange(rows), GRID_W).astype(F32)
    col = jnp.tile(jnp.arange(GRID_W), rows).astype(F32)
    axis_dim = HEAD // 2
    freqs = ROPE_THETA ** (-jnp.arange(0, axis_dim, 2, dtype=F32) / axis_dim)
    ang = jnp.concatenate([row[:, None] * freqs, col[:, None] * freqs], axis=-1)
    cos, sin = jnp.cos(ang), jnp.sin(ang)
    cosf = jnp.repeat(cos, 2, axis=-1)
    sinf = jnp.stack([-sin, sin], axis=-1).reshape(lr, HEAD)
    return cosf, sinf


def kernel(x_prompt, x_sample, meta_tokens, rel_bias_table, ffn1_norm, ffn1_w_in, ffn1_w_out,
           mix_norm, w_in, diff_lambda, diff_subln, qk_norm, w_out,
           ffn2_norm, ffn2_w_in, ffn2_w_out, final_norm):
    w1i, w1o = ffn1_w_in[0].astype(BF16), ffn1_w_out[0].astype(BF16)
    w2i, w2o = ffn2_w_in[0].astype(BF16), ffn2_w_out[0].astype(BF16)
    wi, wo = w_in[0].astype(BF16), w_out[0].astype(BF16)
    g1, gm, g2 = ffn1_norm, mix_norm, ffn2_norm
    gf = final_norm[None, :]
    qkg = qk_norm[0]
    lam_p = diff_lambda[0]
    subln = diff_subln

    bands, cfar, bias_meta = _bias_tables(rel_bias_table)
    mask_b = jnp.where(jnp.arange(META_PAD)[:, None] < N_META, 0.0, NEG).astype(F32)
    mask_b = jnp.broadcast_to(mask_b, (META_PAD, B_GROUP * B_MQ))

    xm = jnp.zeros((META_PAD, D_MODEL), F32).at[:N_META].set(meta_tokens)
    hm = _ffn(xm, g1, w1i, w1o, gf, tm=META_PAD, final_norm=False)
    ones = jnp.ones((META_PAD, HEAD), F32)
    _, kma, vmat, _, kmb, vmbt = _proj(hm, gm, wi, qkg, ones, jnp.zeros_like(ones), tm=META_PAD)

    def trunk(x):
        batch, lr, d = x.shape
        h = _ffn(x.reshape(batch * lr, d), g1, w1i, w1o, gf, tm=FFN_TM, final_norm=False)
        cosf, sinf = _rope_tables(lr)
        qat, ka, vat, qbt, kb, vbt = _proj(h, gm, wi, qkg, cosf, sinf, tm=PROJ_TM)
        ya = _attn_a(cfar, lam_p, subln, qat, ka, vat, kma, vmat, bias_meta, bands, batch=batch)
        yb = _attn_b(qbt, kb, vbt, kmb, vmbt, mask_b, batch=batch)
        h = _out_proj(h, ya, yb, wo, tm=PROJ_TM)
        y = _ffn(h, g2, w2i, w2o, gf, tm=FFN_TM, final_norm=True)
        return y.reshape(batch, lr, d)

    return trunk(x_prompt), trunk(x_sample)
```

```python
import functools
import math

import jax
import jax.numpy as jnp
from jax import lax
from jax.experimental import pallas as pl
from jax.experimental.pallas import tpu as pltpu

F32 = jnp.float32
BF16 = jnp.bfloat16

EPS = 1e-6
N_META = 16
GRID_W = 64
D_MODEL = 2048
A_HEADS = 8
A_QK = 64
HEAD = 128
B_HEADS = 8
B_KV = 2
B_GROUP = B_HEADS // B_KV
A_COLS = A_HEADS * HEAD
REL_BUCKETS = 32
REL_MAX_DIST = 128
ROPE_THETA = 10000.0
LAM_INIT = 0.8 - 0.6 * math.exp(-0.3 * 0)
LOG2E = math.log2(math.e)
A_QSCALE = A_QK ** -0.5 * LOG2E
B_QSCALE = HEAD ** -0.5 * LOG2E
NEG = -1e30

META_PAD = 128
REL_CLIP = 128

VMEM_LIMIT = 60 * 1024 * 1024

FFN_TM = 1024
FFN_TF = 512
FFN_TN = 512
PROJ_TM = 512
KEY_BLOCK = PROJ_TM
A_MQ = 512
B_MQ = 256
A_NEAR = 4
N_BANDS = 2 * A_NEAR - 1
MAX_EXPONENT = 126.0
BOUND_ROWS = 16
BOUND_SLACK = 1.01
MIN_SUM = 2.0 ** -60
FIXED_REF_GROUP = 16


def _rms(x, g):
    return x * lax.rsqrt(jnp.mean(x * x, axis=-1, keepdims=True) + EPS) * g


def _params(n_grid_dims):
    return pltpu.CompilerParams(
        dimension_semantics=("arbitrary",) * n_grid_dims,
        vmem_limit_bytes=VMEM_LIMIT,
    )


def _ffn_kernel(x_ref, g_ref, wg_ref, wu_ref, wo_ref, fg_ref, *rest, final_norm, normed):
    xn_ref, o_ref = rest if normed else rest[::-1]
    j = pl.program_id(1)

    @pl.when(j == 0)
    def _():
        x = x_ref[...]
        if not normed:
            xn_ref[...] = _rms(x, g_ref[...]).astype(BF16)
        o_ref[...] = x

    xn = xn_ref[...]
    gate = jnp.dot(xn, wg_ref[...], preferred_element_type=F32)
    up = jnp.dot(xn, wu_ref[...], preferred_element_type=F32)
    act = (0.5 * gate / (1.0 + jnp.exp(-gate)) * up).astype(BF16)
    d = o_ref.shape[1]
    for n0 in range(0, d, FFN_TN):
        o_ref[:, n0:n0 + FFN_TN] += jnp.dot(act, wo_ref[:, n0:n0 + FFN_TN],
                                            preferred_element_type=F32)

    if final_norm:
        @pl.when(j == pl.num_programs(1) - 1)
        def _():
            o_ref[...] = _rms(o_ref[...], fg_ref[...])


def _ffn(x, norm_g, w_in, w_out, final_g, *, tm, final_norm, xn=None):
    n, d = x.shape
    dff = w_out.shape[0]
    nj = dff // FFN_TF
    normed = xn is not None
    tile = pl.BlockSpec((tm, d), lambda i, j: (i, 0))
    return pl.pallas_call(
        functools.partial(_ffn_kernel, final_norm=final_norm, normed=normed),
        grid=(n // tm, nj),
        in_specs=[
            tile,
            pl.BlockSpec((1, d), lambda i, j: (0, 0)),
            pl.BlockSpec((d, FFN_TF), lambda i, j: (0, j)),
            pl.BlockSpec((d, FFN_TF), lambda i, j: (0, j + nj)),
            pl.BlockSpec((FFN_TF, d), lambda i, j: (j, 0)),
            pl.BlockSpec((1, d), lambda i, j: (0, 0)),
        ] + ([tile] if normed else []),
        out_specs=tile,
        out_shape=jax.ShapeDtypeStruct((n, d), F32),
        scratch_shapes=[] if normed else [pltpu.VMEM((tm, d), BF16)],
        compiler_params=_params(2),
        name="ffn_final" if final_norm else "ffn",
    )(x, norm_g, w_in, w_in, w_out, final_g, *((xn,) if normed else ()))


def _proj_kernel(h_ref, g_ref, w_ref, qkg_ref, cos_ref, sin_ref,
                 qat_ref, ka_ref, vat_ref, qbt_ref, kb_ref, vbt_ref):
    tm = h_ref.shape[0]
    u = _rms(h_ref[...], g_ref[...]).astype(BF16)
    cosf = cos_ref[...]
    sinf = sin_ref[...]
    lane = lax.broadcasted_iota(jnp.int32, (tm, HEAD), 1)
    even = (lane & 1) == 0

    def proj(c0, width):
        return jnp.dot(u, w_ref[:, c0:c0 + width], preferred_element_type=F32)

    def head(y, hh):
        return y[:, hh * HEAD:(hh + 1) * HEAD]

    def norm_rope(y, g):
        y = _rms(y, g)
        swapped = jnp.where(even, pltpu.roll(y, HEAD - 1, 1), pltpu.roll(y, 1, 1))
        return y * cosf + swapped * sinf

    chunk = 4 * HEAD
    for c in range(2):
        y = proj(c * chunk, chunk)
        for hh in range(4):
            qat_ref[c * 4 + hh] = (head(y, hh) * A_QSCALE).T.astype(BF16)
    for c in range(2):
        ka_ref[:, c * chunk:(c + 1) * chunk] = proj(A_COLS + c * chunk, chunk).astype(BF16)
    for c in range(2):
        y = proj(2 * A_COLS + c * chunk, chunk)
        for hh in range(4):
            vat_ref[c * 4 + hh, 0] = head(y, hh).T.astype(BF16)
    gq = qkg_ref[0:1, :]
    gk = qkg_ref[1:2, :]
    for c in range(2):
        y = proj(3 * A_COLS + c * chunk, chunk)
        for hh in range(4):
            qbt_ref[c * 4 + hh] = (norm_rope(head(y, hh), gq) * B_QSCALE).T.astype(BF16)
    y = proj(4 * A_COLS, chunk)
    for n in range(B_KV):
        kb_ref[:, n * HEAD:(n + 1) * HEAD] = norm_rope(head(y, n), gk).astype(BF16)
        vbt_ref[n, 0] = head(y, B_KV + n).T.astype(BF16)


def _proj(h, norm_g, w_in, qk_g, cosf, sinf, *, tm):
    n, d = h.shape
    nt = n // tm
    ntab = cosf.shape[0] // tm
    in_w = w_in.shape[1]
    out_shape = (
        jax.ShapeDtypeStruct((A_HEADS, HEAD, n), BF16),
        jax.ShapeDtypeStruct((n, A_COLS), BF16),
        jax.ShapeDtypeStruct((A_HEADS, nt, HEAD, tm), BF16),
        jax.ShapeDtypeStruct((B_HEADS, HEAD, n), BF16),
        jax.ShapeDtypeStruct((n, B_KV * HEAD), BF16),
        jax.ShapeDtypeStruct((B_KV, nt, HEAD, tm), BF16),
    )
    return pl.pallas_call(
        _proj_kernel,
        grid=(nt,),
        in_specs=[
            pl.BlockSpec((tm, d), lambda i: (i, 0)),
            pl.BlockSpec((1, d), lambda i: (0, 0)),
            pl.BlockSpec((d, in_w), lambda i: (0, 0), pipeline_mode=pl.Buffered(1)),
            pl.BlockSpec((2, HEAD), lambda i: (0, 0)),
            pl.BlockSpec((tm, HEAD), lambda i: (i % ntab, 0)),
            pl.BlockSpec((tm, HEAD), lambda i: (i % ntab, 0)),
        ],
        out_specs=(
            pl.BlockSpec((A_HEADS, HEAD, tm), lambda i: (0, 0, i)),
            pl.BlockSpec((tm, A_COLS), lambda i: (i, 0)),
            pl.BlockSpec((A_HEADS, 1, HEAD, tm), lambda i: (0, i, 0, 0)),
            pl.BlockSpec((B_HEADS, HEAD, tm), lambda i: (0, 0, i)),
            pl.BlockSpec((tm, B_KV * HEAD), lambda i: (i, 0)),
            pl.BlockSpec((B_KV, 1, HEAD, tm), lambda i: (0, i, 0, 0)),
        ),
        out_shape=out_shape,
        compiler_params=_params(1),
        name="mix_proj",
    )(h, norm_g, w_in, qk_g, cosf, sinf)


def _online_block(s, c, vt, m, l, acc_ref):
    m_new = jnp.maximum(m, jnp.max(s, axis=0, keepdims=True) + c)
    alpha = jnp.exp2(m - m_new)
    p = jnp.exp2(s - (m_new - c))
    l_new = alpha * l + jnp.sum(p, axis=0, keepdims=True)
    acc_ref[...] = alpha * acc_ref[...] + jnp.dot(vt, p.astype(BF16), preferred_element_type=F32)
    return m_new, l_new


def _first_block(s, vt, acc_ref):
    m = jnp.max(s, axis=0, keepdims=True)
    p = jnp.exp2(s - m)
    acc_ref[...] = jnp.dot(vt, p.astype(BF16), preferred_element_type=F32)
    return m, jnp.sum(p, axis=0, keepdims=True)


def _fixed_ref_blocks(lead, n_blocks, block_of, key_scores, off_of, vt_ref, acc_ref, p_buf):
    group = FIXED_REF_GROUP
    n_loop = max(n_blocks - 1, 0) // group
    first = n_blocks - n_loop * group
    assert group % 2 == 0 and len(lead) + first >= 2
    acc_started = []

    def values_of(prev):
        if callable(prev):
            return prev
        return lambda slot: jnp.dot(vt_ref[prev], p_buf[slot], preferred_element_type=F32)

    def add_values(prev, slot):
        prod = values_of(prev)(slot)
        if acc_started:
            acc_ref[...] += prod
        else:
            acc_ref[...] = prod
            acc_started.append(True)

    def block(slot, s, off, prev, l):
        if prev is not None:
            add_values(prev, 1 - slot)
        p = jnp.exp2(s - off)
        p_buf[slot, :s.shape[0]] = p.astype(BF16)
        return l + jnp.sum(p, axis=0, keepdims=True)

    l = jnp.zeros((1, acc_ref.shape[1]), F32)
    prev = None
    for n, (scores_fn, off, values_fn) in enumerate(lead):
        l = block(n % 2, scores_fn(), off, prev, l)
        prev = values_fn
    for u in range(first):
        j = block_of(u)
        l = block((len(lead) + u) % 2, key_scores(j), off_of(j), prev, l)
        prev = j

    def body(g, carry):
        l, prev = carry
        for u in range(group):
            j = block_of(first + g * group + u)
            l = block((len(lead) + first + u) % 2, key_scores(j), off_of(j), prev, l)
            prev = j
        return l, prev

    if n_loop:
        l, prev = lax.fori_loop(0, n_loop, body, (l, jnp.asarray(prev, jnp.int32)))
    add_values(prev, (len(lead) + n_blocks - 1) % 2)
    return l


def _key_value_bounds(k_ref, vt_ref, km_ref, vmt_ref, kmax_ref, room_ref):
    nk = vt_ref.shape[0]

    def kbody(kb, kmax):
        start = pl.multiple_of(kb * KEY_BLOCK, KEY_BLOCK)
        blk = jnp.abs(k_ref[pl.ds(start, KEY_BLOCK), :].astype(F32))
        return jnp.maximum(kmax, jnp.max(blk, axis=0, keepdims=True))
    kmax = jnp.max(jnp.abs(km_ref[...].astype(F32)), axis=0, keepdims=True)
    kmax = lax.fori_loop(0, nk, kbody, kmax)
    kmax_ref[...] = jnp.broadcast_to(kmax, kmax_ref.shape).astype(BF16)

    def vbody(j, vmax):
        return jnp.maximum(vmax, jnp.max(jnp.abs(vt_ref[j].astype(F32)), axis=0, keepdims=True))
    vmax = lax.fori_loop(0, nk, vbody, jnp.ones((1, KEY_BLOCK), F32))
    vmeta = jnp.max(jnp.abs(vmt_ref[...].astype(F32)), axis=0, keepdims=True)
    vmax = jnp.maximum(jnp.max(vmax, axis=1, keepdims=True),
                       jnp.max(vmeta, axis=1, keepdims=True))
    room = MAX_EXPONENT - math.log2(nk * KEY_BLOCK + META_PAD) - jnp.log2(vmax)
    room_ref[...] = jnp.broadcast_to(room, room_ref.shape)


def _fixed_reference(kmax_ref, room_ref, qq, c_max):
    ub = jnp.dot(kmax_ref[...], jnp.abs(qq), preferred_element_type=F32)[0:1] * BOUND_SLACK
    return ub + c_max - room_ref[0:1, 0:1]


def _pipelined_blocks(qq, k_ref, vt_ref, c_of, m, l, acc_ref, s_buf, p_buf):
    nk = vt_ref.shape[0]
    kb_size = vt_ref.shape[2]
    assert nk % 2 == 0 and nk >= 2

    def stage_scores(j, slot, m):
        start = pl.multiple_of(j * kb_size, kb_size)
        s = jnp.dot(k_ref[pl.ds(start, kb_size), :], qq, preferred_element_type=F32)
        s_buf[slot] = s
        c = c_of(j)
        m_new = jnp.maximum(m, jnp.max(s, axis=0, keepdims=True) + c)
        return m_new, jnp.exp2(m - m_new), m_new - c

    def stage_exp(slot, off, alpha, l):
        p = jnp.exp2(s_buf[slot] - off)
        p_buf[slot] = p.astype(BF16)
        return alpha * l + jnp.sum(p, axis=0, keepdims=True)

    def stage_values(j, slot, alpha):
        acc_ref[...] = alpha * acc_ref[...] + jnp.dot(vt_ref[j], p_buf[slot],
                                                      preferred_element_type=F32)

    m, a0, o0 = stage_scores(0, 0, m)
    l = stage_exp(0, o0, a0, l)
    m, a1, o1 = stage_scores(1, 1, m)

    def pair(g, carry):
        m, l, a_pp, a_p, o_p = carry
        j = 2 * g
        stage_values(j - 2, 0, a_pp)
        l = stage_exp(1, o_p, a_p, l)
        m, a_j, o_j = stage_scores(j, 0, m)
        stage_values(j - 1, 1, a_p)
        l = stage_exp(0, o_j, a_j, l)
        m, a_j1, o_j1 = stage_scores(j + 1, 1, m)
        return m, l, a_j, a_j1, o_j1

    m, l, a_pp, a_p, o_p = lax.fori_loop(1, nk // 2, pair, (m, l, a0, a1, o1))
    stage_values(nk - 2, 0, a_pp)
    l = stage_exp(1, o_p, a_p, l)
    stage_values(nk - 1, 1, a_p)
    return m, l


def _attn_a_kernel(cfar_ref, lam_ref, g_ref, qt_ref, k_ref, vt_ref, km_ref, vmt_ref, bm_ref,
                   band_ref, o_ref, acc_ref, s_buf, p_buf, kmax_ref, room_ref):
    h = pl.program_id(1)
    i = pl.program_id(2)
    mq = qt_ref.shape[1]
    nk = vt_ref.shape[0]
    kb_size = vt_ref.shape[2]

    @pl.when(i == 0)
    def _():
        _key_value_bounds(k_ref, vt_ref, km_ref, vmt_ref, kmax_ref, room_ref)

    qt = qt_ref[...]
    row = lax.broadcasted_iota(jnp.int32, qt.shape, 0)
    zero = jnp.zeros_like(qt)
    qq = jnp.concatenate([jnp.where(row < A_QK, qt, zero), jnp.where(row >= A_QK, qt, zero)], axis=1)

    lp = lam_ref[...]
    lam = (jnp.exp(jnp.sum(lp[0:1] * lp[1:2], axis=1, keepdims=True))
           - jnp.exp(jnp.sum(lp[2:3] * lp[3:4], axis=1, keepdims=True)) + LAM_INIT)

    def scores(k):
        return jnp.dot(k, qq, preferred_element_type=F32)

    def twice(b):
        return jnp.concatenate([b, b], axis=1)

    def meta_scores():
        return scores(km_ref[...]) + twice(bm_ref[...])

    assert mq == kb_size and nk >= A_NEAR
    near = jnp.clip(i - 1, 0, nk - A_NEAR)

    def key_scores(kb):
        start = pl.multiple_of(kb * kb_size, kb_size)
        return scores(k_ref[pl.ds(start, kb_size), :])

    def band_block(n):
        kb = near + n
        return kb, lambda: key_scores(kb) + twice(band_ref[kb - i + A_NEAR - 1])

    c_lo = cfar_ref[3 * h]
    c_hi = cfar_ref[3 * h + 1]
    c_max = cfar_ref[3 * h + 2]

    m = _fixed_reference(kmax_ref, room_ref, qq, c_max)
    lead = [(meta_scores, m,
             lambda slot: jnp.dot(vmt_ref[...], p_buf[slot, :META_PAD], preferred_element_type=F32))]
    for n in range(A_NEAR):
        kb, scores_fn = band_block(n)
        lead.append((scores_fn, m, kb))
    l = _fixed_ref_blocks(lead, nk - A_NEAR, lambda jj: jj + jnp.where(jj >= near, A_NEAR, 0),
                          key_scores, lambda j: m - jnp.where(j < near, c_lo, c_hi),
                          vt_ref, acc_ref, p_buf)

    def running_reference(_):
        m, l = _first_block(meta_scores(), vmt_ref[...], acc_ref)
        for n in range(A_NEAR):
            kb, scores_fn = band_block(n)
            m, l = _online_block(scores_fn(), 0.0, vt_ref[kb], m, l, acc_ref)

        def c_of(j):
            return jnp.where(j < near, c_lo, jnp.where(j >= near + A_NEAR, c_hi, NEG))
        return _pipelined_blocks(qq, k_ref, vt_ref, c_of, m, l, acc_ref, s_buf, p_buf)[1]

    l = lax.cond(jnp.min(l) >= MIN_SUM, lambda l: l, running_reference, l)

    r = 1.0 / l
    acc = acc_ref[...]
    o = acc[:, :mq] * r[:, :mq] - lam * (acc[:, mq:] * r[:, mq:])
    o_ref[...] = (_rms(o.T, g_ref[...]) * (1.0 - LAM_INIT)).astype(BF16)


def _attn_scratch(width):
    return [pltpu.VMEM((HEAD, width), F32),
            pltpu.VMEM((2, KEY_BLOCK, width), F32),
            pltpu.VMEM((2, KEY_BLOCK, width), BF16),
            pltpu.VMEM((BOUND_ROWS, HEAD), BF16),
            pltpu.VMEM((8, HEAD), F32)]


def _attn_a(cfar, lam_p, subln_g, qat, ka, vat, kma, vmat, bias_meta, bands, *, batch):
    n = ka.shape[0]
    lr = n // batch
    nk = lr // KEY_BLOCK
    nq = lr // A_MQ
    return pl.pallas_call(
        _attn_a_kernel,
        grid=(batch, A_HEADS, nq),
        in_specs=[
            pl.BlockSpec(memory_space=pltpu.SMEM),
            pl.BlockSpec((4, A_QK), lambda b, h, i: (0, 0)),
            pl.BlockSpec((1, HEAD), lambda b, h, i: (0, 0)),
            pl.BlockSpec((None, HEAD, A_MQ), lambda b, h, i: (h, 0, b * nq + i)),
            pl.BlockSpec((lr, HEAD), lambda b, h, i: (b, h)),
            pl.BlockSpec((None, nk, HEAD, KEY_BLOCK), lambda b, h, i: (h, b, 0, 0)),
            pl.BlockSpec((META_PAD, HEAD), lambda b, h, i: (0, h)),
            pl.BlockSpec((None, None, HEAD, META_PAD), lambda b, h, i: (h, 0, 0, 0)),
            pl.BlockSpec((None, None, META_PAD, A_MQ), lambda b, h, i: (h, jnp.minimum(i, 1), 0, 0)),
            pl.BlockSpec((None, N_BANDS, KEY_BLOCK, A_MQ), lambda b, h, i: (h, 0, 0, 0)),
        ],
        out_specs=pl.BlockSpec((A_MQ, HEAD), lambda b, h, i: (b * nq + i, h)),
        out_shape=jax.ShapeDtypeStruct((n, A_COLS), BF16),
        scratch_shapes=_attn_scratch(2 * A_MQ),
        compiler_params=_params(3),
        name="attn_a",
    )(cfar, lam_p, subln_g, qat, ka, vat, kma, vmat, bias_meta, bands)


def _attn_b_kernel(qt_ref, k_ref, vt_ref, km_ref, vmt_ref, mask_ref, o_ref, acc_ref, s_buf, p_buf,
                   kmax_ref, room_ref):
    mq = qt_ref.shape[2]
    nk = vt_ref.shape[0]
    kb_size = vt_ref.shape[2]

    @pl.when(pl.program_id(2) == 0)
    def _():
        _key_value_bounds(k_ref, vt_ref, km_ref, vmt_ref, kmax_ref, room_ref)

    qq = jnp.concatenate([qt_ref[g] for g in range(B_GROUP)], axis=1)
    def meta_scores():
        return jnp.dot(km_ref[...], qq, preferred_element_type=F32) + mask_ref[...]

    def key_scores(j):
        start = pl.multiple_of(j * kb_size, kb_size)
        return jnp.dot(k_ref[pl.ds(start, kb_size), :], qq, preferred_element_type=F32)

    m = _fixed_reference(kmax_ref, room_ref, qq, 0.0)
    lead = [(meta_scores, m,
             lambda slot: jnp.dot(vmt_ref[...], p_buf[slot, :META_PAD], preferred_element_type=F32))]
    l = _fixed_ref_blocks(lead, nk, lambda jj: jj, key_scores, lambda j: m, vt_ref, acc_ref, p_buf)

    def running_reference(_):
        m, l = _first_block(meta_scores(), vmt_ref[...], acc_ref)
        return _pipelined_blocks(qq, k_ref, vt_ref, lambda j: 0.0, m, l, acc_ref, s_buf, p_buf)[1]

    l = lax.cond(jnp.min(l) >= MIN_SUM, lambda l: l, running_reference, l)
    o = acc_ref[...] * (1.0 / l)
    for g in range(B_GROUP):
        o_ref[:, g * HEAD:(g + 1) * HEAD] = o[:, g * mq:(g + 1) * mq].T.astype(BF16)


def _attn_b(qbt, kb, vbt, kmb, vmbt, mask, *, batch):
    n = kb.shape[0]
    lr = n // batch
    nk = lr // KEY_BLOCK
    nq = lr // B_MQ
    return pl.pallas_call(
        _attn_b_kernel,
        grid=(batch, B_KV, nq),
        in_specs=[
            pl.BlockSpec((B_GROUP, HEAD, B_MQ), lambda b, n_, i: (n_, 0, b * nq + i)),
            pl.BlockSpec((lr, HEAD), lambda b, n_, i: (b, n_)),
            pl.BlockSpec((None, nk, HEAD, KEY_BLOCK), lambda b, n_, i: (n_, b, 0, 0)),
            pl.BlockSpec((META_PAD, HEAD), lambda b, n_, i: (0, n_)),
            pl.BlockSpec((None, None, HEAD, META_PAD), lambda b, n_, i: (n_, 0, 0, 0)),
            pl.BlockSpec((META_PAD, B_GROUP * B_MQ), lambda b, n_, i: (0, 0)),
        ],
        out_specs=pl.BlockSpec((B_MQ, B_GROUP * HEAD), lambda b, n_, i: (b * nq + i, n_)),
        out_shape=jax.ShapeDtypeStruct((n, B_HEADS * HEAD), BF16),
        scratch_shapes=_attn_scratch(B_GROUP * B_MQ),
        compiler_params=_params(3),
        name="attn_b",
    )(qbt, kb, vbt, kmb, vmbt, mask)


def _out_kernel(h_ref, ya_ref, yb_ref, w_ref, g_ref, o_ref, xn_ref):
    o = (h_ref[...]
         + jnp.dot(ya_ref[...], w_ref[:A_COLS, :], preferred_element_type=F32)
         + jnp.dot(yb_ref[...], w_ref[A_COLS:, :], preferred_element_type=F32))
    o_ref[...] = o
    xn_ref[...] = _rms(o, g_ref[...]).astype(BF16)


def _out_proj(h, ya, yb, w_out, next_norm_g, *, tm):
    n, d = h.shape
    return pl.pallas_call(
        _out_kernel,
        grid=(n // tm,),
        in_specs=[
            pl.BlockSpec((tm, d), lambda i: (i, 0)),
            pl.BlockSpec((tm, A_COLS), lambda i: (i, 0)),
            pl.BlockSpec((tm, B_HEADS * HEAD), lambda i: (i, 0)),
            pl.BlockSpec(w_out.shape, lambda i: (0, 0), pipeline_mode=pl.Buffered(1)),
            pl.BlockSpec((1, d), lambda i: (0, 0)),
        ],
        out_specs=(pl.BlockSpec((tm, d), lambda i: (i, 0)), pl.BlockSpec((tm, d), lambda i: (i, 0))),
        out_shape=(jax.ShapeDtypeStruct((n, d), F32), jax.ShapeDtypeStruct((n, d), BF16)),
        compiler_params=_params(1),
        name="out_proj",
    )(h, ya, yb, w_out, next_norm_g)


def _t5_bucket(rel):
    half = REL_BUCKETS // 2
    max_exact = half // 2
    n = jnp.abs(rel)
    sign_off = jnp.where(rel > 0, half, 0)
    nf = jnp.maximum(n, 1).astype(F32)
    large = max_exact + (jnp.log(nf / max_exact) / math.log(REL_MAX_DIST / max_exact)
                         * (half - max_exact)).astype(jnp.int32)
    large = jnp.minimum(large, half - 1)
    return sign_off + jnp.where(n < max_exact, n, large)


def _toeplitz_kernel(vec_ref, o_ref):
    rows, cols = o_ref.shape
    x = jnp.broadcast_to(vec_ref[...], (rows, vec_ref.shape[1]))
    o_ref[...] = pltpu.roll(x, 0, 1, stride=1, stride_axis=0)[:, :cols]


def _toeplitz(f, rows, cols):
    period = rows + cols
    x = jnp.arange(period)
    vec = f(jnp.where(x < cols, -x, period - x))
    n = vec.shape[0]
    return pl.pallas_call(
        _toeplitz_kernel,
        grid=(n,),
        in_specs=[pl.BlockSpec((None, 1, period), lambda i: (i, 0, 0))],
        out_specs=pl.BlockSpec((None, rows, cols), lambda i: (i, 0, 0)),
        out_shape=jax.ShapeDtypeStruct((n, rows, cols), F32),
        compiler_params=_params(1),
        name="bias_bands",
    )(vec[:, None, :])


def _bias_tables(rel_table):
    rel1d = jnp.arange(-REL_CLIP, REL_CLIP + 1)
    t1d = rel_table.astype(F32)[_t5_bucket(rel1d)].T * LOG2E

    def lookup(rel):
        return t1d[:, jnp.clip(rel, -REL_CLIP, REL_CLIP) + REL_CLIP]

    delta = ((jnp.arange(N_BANDS) - (A_NEAR - 1)) * KEY_BLOCK)[:, None]
    bands = _toeplitz(lambda y: lookup(delta + y[None, :]).reshape(A_HEADS * N_BANDS, -1),
                      KEY_BLOCK, A_MQ).reshape(A_HEADS, N_BANDS, KEY_BLOCK, A_MQ)

    c_lo = t1d[:, 0]
    c_hi = t1d[:, -1]
    cfar = jnp.stack([c_lo, c_hi, jnp.max(t1d, axis=1)], axis=1).reshape(-1)

    j = jnp.arange(META_PAD)[:, None]
    first = _toeplitz(lambda y: lookup(y - N_META), META_PAD, A_MQ)
    rest = jnp.broadcast_to(c_lo[:, None, None], first.shape)
    bias_meta = jnp.where((j < N_META)[None], jnp.stack([first, rest], axis=1), NEG)
    return bands, cfar, bias_meta


def _rope_tables(lr):
    rows = lr // GRID_W
    row = jnp.repeat(jnp.arange(rows), GRID_W).astype(F32)
    col = jnp.tile(jnp.arange(GRID_W), rows).astype(F32)
    axis_dim = HEAD // 2
    freqs = ROPE_THETA ** (-jnp.arange(0, axis_dim, 2, dtype=F32) / axis_dim)
    ang = jnp.concatenate([row[:, None] * freqs, col[:, None] * freqs], axis=-1)
    cos, sin = jnp.cos(ang), jnp.sin(ang)
    cosf = jnp.repeat(cos, 2, axis=-1)
    sinf = jnp.stack([-sin, sin], axis=-1).reshape(lr, HEAD)
    return cosf, sinf


def kernel(x_prompt, x_sample, meta_tokens, rel_bias_table, ffn1_norm, ffn1_w_in, ffn1_w_out,
           mix_norm, w_in, diff_lambda, diff_subln, qk_norm, w_out,
           ffn2_norm, ffn2_w_in, ffn2_w_out, final_norm):
    w1i, w1o = ffn1_w_in[0].astype(BF16), ffn1_w_out[0].astype(BF16)
    w2i, w2o = ffn2_w_in[0].astype(BF16), ffn2_w_out[0].astype(BF16)
    wi, wo = w_in[0].astype(BF16), w_out[0].astype(BF16)
    g1, gm, g2 = ffn1_norm, mix_norm, ffn2_norm
    gf = final_norm[None, :]
    qkg = qk_norm[0]
    lam_p = diff_lambda[0]
    subln = diff_subln

    bands, cfar, bias_meta = _bias_tables(rel_bias_table)
    mask_b = jnp.where(jnp.arange(META_PAD)[:, None] < N_META, 0.0, NEG).astype(F32)
    mask_b = jnp.broadcast_to(mask_b, (META_PAD, B_GROUP * B_MQ))

    xm = jnp.zeros((META_PAD, D_MODEL), F32).at[:N_META].set(meta_tokens)
    hm = _ffn(xm, g1, w1i, w1o, gf, tm=META_PAD, final_norm=False)
    ones = jnp.ones((META_PAD, HEAD), F32)
    _, kma, vmat, _, kmb, vmbt = _proj(hm, gm, wi, qkg, ones, jnp.zeros_like(ones), tm=META_PAD)

    def trunk(x):
        batch, lr, d = x.shape
        h = _ffn(x.reshape(batch * lr, d), g1, w1i, w1o, gf, tm=FFN_TM, final_norm=False)
        cosf, sinf = _rope_tables(lr)
        qat, ka, vat, qbt, kb, vbt = _proj(h, gm, wi, qkg, cosf, sinf, tm=PROJ_TM)
        ya = _attn_a(cfar, lam_p, subln, qat, ka, vat, kma, vmat, bias_meta, bands, batch=batch)
        yb = _attn_b(qbt, kb, vbt, kmb, vmbt, mask_b, batch=batch)
        h, hn = _out_proj(h, ya, yb, wo, g2, tm=PROJ_TM)
        y = _ffn(h, g2, w2i, w2o, gf, tm=FFN_TM, final_norm=True, xn=hn)
        return y.reshape(batch, lr, d)

    return trunk(x_prompt), trunk(x_sample)
```
